```python
import math
import jax, jax.numpy as jnp
from jax import lax
import numpy as np

D_MODEL = 1024
BATCH = 2
SEQ = 16384
DEPTH = 4

D_MIX = D_MODEL
D_MLSTM = D_MIX // 2
D_S5 = D_MIX - D_MLSTM
MLSTM_HEADS = 4
MLSTM_HEAD_DIM = D_MLSTM // MLSTM_HEADS
MLSTM_CHUNK = 64
CONV_WIDTH = 4
S5_GROUP = 16
S5_GROUPS = D_S5 // S5_GROUP
S5_STATE = 64
D_FF_DENSE = 2816
N_EXPERTS = 8
TOP_K = 2
D_FF_EXPERT = 3584
N_DENSE = (DEPTH + 1) // 2
N_MOE = DEPTH // 2
RMS_EPS = 1e-6
PROJ_SPLITS = (D_MLSTM, 2 * D_MLSTM, 3 * D_MLSTM, 4 * D_MLSTM,
               4 * D_MLSTM + MLSTM_HEADS, 4 * D_MLSTM + 2 * MLSTM_HEADS)
PROJ_COLS = 4 * D_MLSTM + 2 * MLSTM_HEADS + D_S5

kernel_name = "hybrid_mlstm_s5_moe_trunk"


def rmsnorm(x, g):
    xf = x.astype(jnp.float32)
    y = xf * lax.rsqrt(jnp.mean(xf * xf, axis=-1, keepdims=True) + RMS_EPS)
    return (y * g.astype(jnp.float32)).astype(x.dtype)


def causal_dwconv(x, w):
    c = x.shape[-1]
    return lax.conv_general_dilated(
        x, w[:, None, :].astype(x.dtype), window_strides=(1,),
        padding=((w.shape[0] - 1, 0),),
        dimension_numbers=("NWC", "WIO", "NWC"), feature_group_count=c)


def _mlstm_chunk(carry, xs):
    c_state, n_state, m_state = carry
    q, k, v, lf, li = xs
    L = q.shape[2]
    b = jnp.cumsum(lf, axis=-1)
    causal = jnp.tril(jnp.ones((L, L), dtype=bool))
    log_d = jnp.where(causal, b[..., :, None] - b[..., None, :] + li[..., None, :], -jnp.inf)
    m_inter = b + m_state[..., None]
    m_t = jnp.maximum(m_inter, jnp.max(log_d, axis=-1))
    d = jnp.exp(log_d - m_t[..., None])
    inter = jnp.exp(m_inter - m_t)
    s = jnp.einsum('bhtd,bhsd->bhts', q, k) * d
    num = jnp.einsum('bhts,bhse->bhte', s, v) + inter[..., None] * jnp.einsum('bhtd,bhde->bhte', q, c_state)
    den = jnp.sum(s, axis=-1) + inter * jnp.einsum('bhtd,bhd->bht', q, n_state)
    h = num / jnp.maximum(jnp.abs(den), jnp.exp(-m_t))[..., None]
    b_end = b[..., -1]
    log_w = b_end[..., None] - b + li
    m_new = jnp.maximum(b_end + m_state, jnp.max(log_w, axis=-1))
    w = jnp.exp(log_w - m_new[..., None])
    decay = jnp.exp(b_end + m_state - m_new)
    c_new = decay[..., None, None] * c_state + jnp.einsum('bhs,bhsd,bhse->bhde', w, k, v)
    n_new = decay[..., None] * n_state + jnp.einsum('bhs,bhsd->bhd', w, k)
    return (c_new, n_new, m_new), h


def mlstm(q, k, v, li, lf):
    B, S, H, dh = q.shape
    nc = S // MLSTM_CHUNK
    to_chunks = lambda t: t.reshape(B, nc, MLSTM_CHUNK, H, dh).transpose(1, 0, 3, 2, 4)
    g_chunks = lambda t: t.reshape(B, nc, MLSTM_CHUNK, H).transpose(1, 0, 3, 2)
    init = (jnp.zeros((B, H, dh, dh), jnp.float32), jnp.zeros((B, H, dh), jnp.float32),
            jnp.zeros((B, H), jnp.float32))
    _, h = lax.scan(_mlstm_chunk, init,
                    (to_chunks(q), to_chunks(k), to_chunks(v), g_chunks(lf), g_chunks(li)))
    return h.transpose(1, 0, 3, 2, 4).reshape(B, S, H, dh)


def s5_ssm(u, a_re, a_im, log_dt, b_re, b_im, c_re, c_im, d_skip):
    S = u.shape[1]
    dt = jnp.exp(log_dt)[:, None]
    mag = jnp.exp(a_re * dt)
    ab_re = mag * jnp.cos(a_im * dt)
    ab_im = mag * jnp.sin(a_im * dt)
    lam2 = a_re * a_re + a_im * a_im
    z_re = ab_re - 1.0
    f_re = (z_re * a_re + ab_im * a_im) / lam2
    f_im = (ab_im * a_re - z_re * a_im) / lam2
    bb_re = f_re[..., None] * b_re - f_im[..., None] * b_im
    bb_im = f_re[..., None] * b_im + f_im[..., None] * b_re
    bu_re = jnp.einsum('gph,bsgh->bsgp', bb_re, u)
    bu_im = jnp.einsum('gph,bsgh->bsgp', bb_im, u)
    shape_a = (1, S) + ab_re.shape
    a_seq_re = jnp.broadcast_to(ab_re[None, None], shape_a)
    a_seq_im = jnp.broadcast_to(ab_im[None, None], shape_a)

    def combine(e1, e2):
        a1r, a1i, b1r, b1i = e1
        a2r, a2i, b2r, b2i = e2
        return (a2r * a1r - a2i * a1i, a2r * a1i + a2i * a1r,
                a2r * b1r - a2i * b1i + b2r, a2r * b1i + a2i * b1r + b2i)

    _, _, x_re, x_im = lax.associative_scan(combine, (a_seq_re, a_seq_im, bu_re, bu_im), axis=1)
    y = (jnp.einsum('ghp,bsgp->bsgh', c_re, x_re) - jnp.einsum('ghp,bsgp->bsgh', c_im, x_im)
         + d_skip * u)
    return y


def hybrid_mixer(h, w_in, conv_qk, b_igate, b_fgate, norm_mlstm, a_re, a_im, log_dt,
                 b_re, b_im, c_re, c_im, d_skip, w_glu, norm_s5, w_out):
    B, S, _ = h.shape
    f32 = jnp.float32
    proj = h @ w_in
    q_pre, k_pre, v, o_pre, i_pre, f_pre, u = jnp.split(proj, PROJ_SPLITS, axis=-1)
    qk = jax.nn.silu(causal_dwconv(jnp.concatenate([q_pre, k_pre], axis=-1), conv_qk)).astype(f32)
    hd = (B, S, MLSTM_HEADS, MLSTM_HEAD_DIM)
    q = qk[..., :D_MLSTM].reshape(hd)
    k = qk[..., D_MLSTM:].reshape(hd) * (MLSTM_HEAD_DIM ** -0.5)
    vv = v.astype(f32).reshape(hd)
    li = i_pre.astype(f32) + b_igate.astype(f32)
    lf = jax.nn.log_sigmoid(f_pre.astype(f32) + b_fgate.astype(f32))
    hm = mlstm(q, k, vv, li, lf)
    hm = hm * lax.rsqrt(jnp.mean(hm * hm, axis=-1, keepdims=True) + RMS_EPS)
    hm = hm.reshape(B, S, D_MLSTM) * norm_mlstm.astype(f32)
    hm = jax.nn.sigmoid(o_pre.astype(f32)) * hm
    ug = u.astype(f32).reshape(B, S, S5_GROUPS, S5_GROUP)
    ys = s5_ssm(ug, a_re.astype(f32), a_im.astype(f32), log_dt.astype(f32),
                b_re.astype(f32), b_im.astype(f32), c_re.astype(f32), c_im.astype(f32),
                d_skip.astype(f32).reshape(S5_GROUPS, S5_GROUP)).reshape(B, S, D_S5)
    g = jax.nn.gelu(ys)
    ys = g * jax.nn.sigmoid(g @ w_glu.astype(f32))
    ys = ys * lax.rsqrt(jnp.mean(ys * ys, axis=-1, keepdims=True) + RMS_EPS) * norm_s5.astype(f32)
    mixed = jnp.concatenate([hm, ys], axis=-1).astype(h.dtype)
    return mixed @ w_out


def swiglu(x, w_gate, w_up, w_down):
    return (jax.nn.silu(x @ w_gate) * (x @ w_up)) @ w_down


def moe_swiglu(x, w_router, b_router, w_gate, w_up, w_down):
    B, S, D = x.shape
    xt = x.reshape(B * S, D)
    logits = (xt @ w_router).astype(jnp.float32) + b_router.astype(jnp.float32)
    top_v, top_i = lax.top_k(logits, TOP_K)
    gates = jax.nn.softmax(top_v, axis=-1)
    combine = jnp.einsum('tk,tke->te', gates, jax.nn.one_hot(top_i, N_EXPERTS, dtype=jnp.float32))
    out = jnp.zeros_like(xt)
    for e in range(N_EXPERTS):
        out = out + combine[:, e:e + 1].astype(xt.dtype) * swiglu(xt, w_gate[e], w_up[e], w_down[e])
    return out.reshape(B, S, D)


def setup_inputs(seed: int = 0) -> dict:
    key = jax.random.key(seed)
    ks = jax.random.split(key, 32)
    f32 = jnp.float32
    nrm = lambda k, shape, scale: jax.random.normal(k, shape, f32) * scale
    n_idx = jnp.arange(S5_STATE, dtype=f32)
    return {
        "x": nrm(ks[0], (BATCH, SEQ, D_MODEL), 1.0),
        "norm_mix": 1.0 + nrm(ks[1], (DEPTH, D_MODEL), 0.02),
        "w_in": nrm(ks[2], (DEPTH, D_MODEL, PROJ_COLS), D_MODEL ** -0.5),
        "conv_qk": nrm(ks[3], (DEPTH, CONV_WIDTH, 2 * D_MLSTM), CONV_WIDTH ** -0.5),
        "b_igate": nrm(ks[4], (DEPTH, MLSTM_HEADS), 0.1),
        "b_fgate": jnp.linspace(3.0, 6.0, MLSTM_HEADS, dtype=f32)[None, :] + nrm(ks[5], (DEPTH, MLSTM_HEADS), 0.1),
        "norm_mlstm": 1.0 + nrm(ks[6], (DEPTH, D_MLSTM), 0.02),
        "s5_a_re": -0.5 + nrm(ks[7], (DEPTH, S5_GROUPS, S5_STATE), 1e-3),
        "s5_a_im": jnp.pi * n_idx + nrm(ks[8], (DEPTH, S5_GROUPS, S5_STATE), 1e-3),
        "s5_log_dt": jax.random.uniform(ks[9], (DEPTH, S5_GROUPS), f32, math.log(1e-3), math.log(1e-1)),
        "s5_b_re": nrm(ks[10], (DEPTH, S5_GROUPS, S5_STATE, S5_GROUP), (2 * S5_GROUP) ** -0.5),
        "s5_b_im": nrm(ks[11], (DEPTH, S5_GROUPS, S5_STATE, S5_GROUP), (2 * S5_GROUP) ** -0.5),
        "s5_c_re": nrm(ks[12], (DEPTH, S5_GROUPS, S5_GROUP, S5_STATE), (2 * S5_STATE) ** -0.5),
        "s5_c_im": nrm(ks[13], (DEPTH, S5_GROUPS, S5_GROUP, S5_STATE), (2 * S5_STATE) ** -0.5),
        "s5_d": nrm(ks[14], (DEPTH, D_S5), 1.0),
        "w_glu": nrm(ks[15], (DEPTH, D_S5, D_S5), D_S5 ** -0.5),
        "norm_s5": 1.0 + nrm(ks[16], (DEPTH, D_S5), 0.02),
        "w_out": nrm(ks[17], (DEPTH, D_MIX, D_MODEL), D_MIX ** -0.5),
        "norm_ffn": 1.0 + nrm(ks[18], (DEPTH, D_MODEL), 0.02),
        "ffn_w_gate": nrm(ks[19], (N_DENSE, D_MODEL, D_FF_DENSE), D_MODEL ** -0.5),
        "ffn_w_up": nrm(ks[20], (N_DENSE, D_MODEL, D_FF_DENSE), D_MODEL ** -0.5),
        "ffn_w_down": nrm(ks[21], (N_DENSE, D_FF_DENSE, D_MODEL), D_FF_DENSE ** -0.5),
        "moe_w_router": nrm(ks[22], (N_MOE, D_MODEL, N_EXPERTS), D_MODEL ** -0.5),
        "moe_b_router": nrm(ks[23], (N_MOE, N_EXPERTS), 0.01),
        "moe_w_gate": nrm(ks[24], (N_MOE, N_EXPERTS, D_MODEL, D_FF_EXPERT), D_MODEL ** -0.5),
        "moe_w_up": nrm(ks[25], (N_MOE, N_EXPERTS, D_MODEL, D_FF_EXPERT), D_MODEL ** -0.5),
        "moe_w_down": nrm(ks[26], (N_MOE, N_EXPERTS, D_FF_EXPERT, D_MODEL), D_FF_EXPERT ** -0.5),
        "norm_final": 1.0 + nrm(ks[27], (D_MODEL,), 0.02),
    }


def reference(x, norm_mix, w_in, conv_qk, b_igate, b_fgate, norm_mlstm, s5_a_re, s5_a_im,
              s5_log_dt, s5_b_re, s5_b_im, s5_c_re, s5_c_im, s5_d, w_glu, norm_s5, w_out,
              norm_ffn, ffn_w_gate, ffn_w_up, ffn_w_down, moe_w_router, moe_b_router,
              moe_w_gate, moe_w_up, moe_w_down, norm_final):
    for l in range(DEPTH):
        hn = rmsnorm(x, norm_mix[l])
        x = x + hybrid_mixer(hn, w_in[l], conv_qk[l], b_igate[l], b_fgate[l], norm_mlstm[l],
                             s5_a_re[l], s5_a_im[l], s5_log_dt[l], s5_b_re[l], s5_b_im[l],
                             s5_c_re[l], s5_c_im[l], s5_d[l], w_glu[l], norm_s5[l], w_out[l])
        hn = rmsnorm(x, norm_ffn[l])
        j = l // 2
        if l % 2 == 0:
            x = x + swiglu(hn, ffn_w_gate[j], ffn_w_up[j], ffn_w_down[j])
        else:
            x = x + moe_swiglu(hn, moe_w_router[j], moe_b_router[j], moe_w_gate[j],
                               moe_w_up[j], moe_w_down[j])
    return rmsnorm(x, norm_final)
```

```python
import functools

import jax
import jax.numpy as jnp
from jax import lax
from jax.experimental import pallas as pl
from jax.experimental.pallas import tpu as pltpu

F32, BF16, I32 = jnp.float32, jnp.bfloat16, jnp.int32
HIGHEST = lax.Precision.HIGHEST

RMS_EPS = 1e-6
D_MODEL = 1024
D_MLSTM = 512
D_S5 = 512
HEADS = 4
HEAD_DIM = 128
CONV_WIDTH = 4
S5_GROUP = 16
S5_GROUPS = 32
S5_STATE = 64
N_EXPERTS = 8
N_GATE_COLS = 2 * HEADS

LANES = 128
SUBLANES = 8
VMEM_LIMIT = 56 * 1024 * 1024

S5_FOLD = SUBLANES
S5_GROUPS_PER_ROW = S5_GROUPS // S5_FOLD
S5_HALF = S5_GROUPS_PER_ROW * S5_STATE
S5_CH_PER_ROW = S5_GROUPS_PER_ROW * S5_GROUP


def _params(sem):
    return pltpu.CompilerParams(dimension_semantics=sem, vmem_limit_bytes=VMEM_LIMIT)


def _const_spec(shape):
    nd = len(shape)
    return pl.BlockSpec(shape, lambda *_: (0,) * nd, pipeline_mode=pl.Buffered(1))


def _rmsnorm(x, g):
    y = x * lax.rsqrt(jnp.mean(x * x, axis=-1, keepdims=True) + RMS_EPS)
    return y * g


def _log_sigmoid(x):
    return jnp.minimum(x, 0.0) - jnp.log1p(jnp.exp(-jnp.abs(x)))


def _inproj_kernel(x_ref, g_ref, wqk_ref, wv_ref, wo_ref, wu_ref, wg_ref, bg_ref,
                   qk_ref, v_ref, o_ref, u_ref, gate_ref):
    hn = _rmsnorm(x_ref[...], g_ref[...]).astype(BF16)
    qk_ref[...] = jnp.dot(hn, wqk_ref[...], preferred_element_type=F32).astype(BF16)
    v_ref[...] = jnp.dot(hn, wv_ref[...], preferred_element_type=F32).astype(BF16)
    o_ref[...] = jnp.dot(hn, wo_ref[...], preferred_element_type=F32).astype(BF16)
    u_ref[...] = jnp.dot(hn, wu_ref[...], preferred_element_type=F32).astype(BF16)
    gp = jnp.dot(hn, wg_ref[...], preferred_element_type=F32) + bg_ref[...]
    lane = lax.broadcasted_iota(I32, gp.shape, 1)
    gate_ref[...] = jnp.where(lane < HEADS, gp, _log_sigmoid(gp))


def _inproj(x2, g, wqk, wv, wo, wu, wg, bg, tm):
    t = x2.shape[0]
    row = lambda n: pl.BlockSpec((tm, n), lambda i: (i, 0))
    return pl.pallas_call(
        _inproj_kernel,
        grid=(t // tm,),
        in_specs=[row(D_MODEL), _const_spec((1, D_MODEL)), _const_spec(wqk.shape), _const_spec(wv.shape),
                  _const_spec(wo.shape), _const_spec(wu.shape), _const_spec(wg.shape), _const_spec((1, LANES))],
        out_specs=[row(2 * D_MLSTM), row(D_MLSTM), row(D_MLSTM), row(D_S5), row(LANES)],
        out_shape=[jax.ShapeDtypeStruct((t, 2 * D_MLSTM), BF16), jax.ShapeDtypeStruct((t, D_MLSTM), BF16),
                   jax.ShapeDtypeStruct((t, D_MLSTM), BF16), jax.ShapeDtypeStruct((t, D_S5), BF16),
                   jax.ShapeDtypeStruct((t, LANES), F32)],
        compiler_params=_params(("arbitrary",)),
        name="inproj",
    )(x2, g, wqk, wv, wo, wu, wg, bg)


CONV_ROWS = 32
HALO = SUBLANES


def _mlstm_kernel(qk_ref, v_ref, o_ref, g_ref, cw_ref, nw_ref, out_ref,
                  xbuf, qkact, cstate, mstate, *, nb, lt, chunk):
    i = pl.program_id(0)

    @pl.when(i == 0)
    def _init():
        xbuf[:, 0:HALO, :] = jnp.zeros((nb, HALO, 2 * D_MLSTM), F32)
        cstate[...] = jnp.zeros(cstate.shape, F32)
        mstate[...] = jnp.zeros(mstate.shape, F32)

    lane = lax.broadcasted_iota(I32, (1, 2 * D_MLSTM), 1)
    qk_scale = jnp.where(lane < D_MLSTM, 1.0, HEAD_DIM ** -0.5).astype(F32)
    for b in range(nb):
        xbuf[b, HALO:HALO + lt, :] = qk_ref[b].astype(F32)

        def conv_block(r, carry, b=b):
            r0 = pl.multiple_of(r * CONV_ROWS, CONV_ROWS)
            win = xbuf[b, pl.ds(r0, CONV_ROWS + HALO), :]
            acc = win[HALO:, :] * cw_ref[CONV_WIDTH - 1:CONV_WIDTH, :]
            for j in range(1, CONV_WIDTH):
                acc = acc + win[HALO - j:HALO - j + CONV_ROWS, :] * cw_ref[CONV_WIDTH - 1 - j:CONV_WIDTH - j, :]
            act = acc * jax.nn.sigmoid(acc) * qk_scale
            qkact[b, pl.ds(r0, CONV_ROWS), :] = act.astype(BF16)
            return carry

        lax.fori_loop(0, lt // CONV_ROWS, conv_block, 0)
        xbuf[b, 0:HALO, :] = xbuf[b, lt:lt + HALO, :]

    L = chunk
    rr = lax.broadcasted_iota(I32, (L, L), 0)
    cc = lax.broadcasted_iota(I32, (L, L), 1)
    causal = rr >= cc
    tri = causal.astype(F32)
    ones_col = (lax.broadcasted_iota(I32, (L, HEAD_DIM), 1) == 0).astype(BF16)

    def chunk_body(c, carry):
        r0 = pl.multiple_of(c * L, L)
        for b in range(nb):
            gates = g_ref[b, pl.ds(r0, L), :]
            csum = jnp.dot(tri, gates, preferred_element_type=F32, precision=HIGHEST)
            gates_t = gates.T
            csum_t = csum.T
            for h in range(HEADS):
                sl = slice(h * HEAD_DIM, (h + 1) * HEAD_DIM)
                slk = slice(D_MLSTM + h * HEAD_DIM, D_MLSTM + (h + 1) * HEAD_DIM)
                li_col = gates[:, h:h + 1]
                li_row = gates_t[h:h + 1, :]
                b_col = csum[:, HEADS + h:HEADS + h + 1]
                b_row = csum_t[HEADS + h:HEADS + h + 1, :]
                m_prev = mstate[b * HEADS + h, 0:1, 0:1]

                log_d = jnp.where(causal, b_col - b_row + li_row, -jnp.inf)
                m_inter = b_col + m_prev
                m_t = jnp.maximum(m_inter, jnp.max(log_d, axis=-1, keepdims=True))
                d = jnp.exp(log_d - m_t)
                inter = jnp.exp(m_inter - m_t)

                q = qkact[b, pl.ds(r0, L), sl]
                k = qkact[b, pl.ds(r0, L), slk]
                v = v_ref[b, pl.ds(r0, L), sl]
                vext = jnp.concatenate([v, ones_col], axis=1)
                s = lax.dot_general(q, k, (((1,), (1,)), ((), ())), preferred_element_type=F32)
                p = (s * d).astype(BF16)
                cext = cstate[b * HEADS + h]
                r = (jnp.dot(p, vext, preferred_element_type=F32)
                     + inter * jnp.dot(q, cext.astype(BF16), preferred_element_type=F32))
                num = r[:, :HEAD_DIM]
                den = r[:, HEAD_DIM:HEAD_DIM + 1]
                hh = num / jnp.maximum(jnp.abs(den), jnp.exp(-m_t))
                hh = hh * lax.rsqrt(jnp.mean(hh * hh, axis=-1, keepdims=True) + RMS_EPS)
                hh = hh * nw_ref[:, sl]
                og = jax.nn.sigmoid(o_ref[b, pl.ds(r0, L), sl].astype(F32))
                out_ref[b, pl.ds(r0, L), sl] = (og * hh).astype(BF16)

                b_end = b_col[L - 1:L, :]
                log_w = b_end - b_col + li_col
                m_new = jnp.maximum(b_end + m_prev, jnp.max(log_w, axis=0, keepdims=True))
                w = jnp.exp(log_w - m_new)
                decay = jnp.exp(b_end + m_prev - m_new)
                kw = (k.astype(F32) * w).astype(BF16)
                upd = lax.dot_general(kw, vext, (((0,), (0,)), ((), ())), preferred_element_type=F32)
                cstate[b * HEADS + h] = decay * cext + upd
                mstate[b * HEADS + h] = jnp.broadcast_to(m_new, (SUBLANES, LANES))
        return carry

    lax.fori_loop(0, lt // L, chunk_body, 0)


def _mlstm(qk, v, o, gates, conv_w, norm_w, lt, chunk):
    nb, s, _ = qk.shape
    blk = lambda n: pl.BlockSpec((nb, lt, n), lambda i: (0, i, 0))
    kern = functools.partial(_mlstm_kernel, nb=nb, lt=lt, chunk=chunk)
    return pl.pallas_call(
        kern,
        grid=(s // lt,),
        in_specs=[blk(2 * D_MLSTM), blk(D_MLSTM), blk(D_MLSTM), blk(LANES),
                  _const_spec((CONV_WIDTH, 2 * D_MLSTM)), _const_spec((1, D_MLSTM))],
        out_specs=blk(D_MLSTM),
        out_shape=jax.ShapeDtypeStruct((nb, s, D_MLSTM), BF16),
        scratch_shapes=[pltpu.VMEM((nb, lt + HALO, 2 * D_MLSTM), F32),
                        pltpu.VMEM((nb, lt, 2 * D_MLSTM), BF16),
                        pltpu.VMEM((nb * HEADS, HEAD_DIM, 2 * HEAD_DIM), F32),
                        pltpu.VMEM((nb * HEADS, SUBLANES, LANES), F32)],
        compiler_params=_params(("arbitrary",)),
        name="mlstm",
    )(qk, v, o, gates, conv_w, norm_w)


S5_SUB = 128
S5_UNROLL = 8


def _gelu_tanh(x):
    return 0.5 * x * (1.0 + jnp.tanh(0.7978845608028654 * (x + 0.044715 * (x * x * x))))


def _s5_kernel(u_ref, wb_ref, are_ref, aim_ref, wc_ref, dsk_ref, wglu_ref, nw_ref, out_ref,
               bu, st, *, nb, lt):
    i = pl.program_id(0)

    @pl.when(i == 0)
    def _init():
        st[...] = jnp.zeros(st.shape, F32)

    rows = S5_SUB * S5_FOLD
    ri = lax.broadcasted_iota(I32, (rows, D_S5), 0) % S5_FOLD
    ci = lax.broadcasted_iota(I32, (rows, D_S5), 1) // S5_CH_PER_ROW
    fold_mask = ri == ci
    nsub = lt // S5_SUB

    for b in range(nb):
        def bproj(s, carry, b=b):
            t0 = pl.multiple_of(s * S5_SUB, S5_SUB)
            us = u_ref[b, pl.ds(t0, S5_SUB), :].astype(F32)
            ex = jnp.broadcast_to(us[:, None, :], (S5_SUB, S5_FOLD, D_S5)).reshape(rows, D_S5)
            ex = jnp.where(fold_mask, ex, 0.0).astype(BF16)
            bu[b, pl.ds(pl.multiple_of(s * rows, rows), rows), :] = jnp.dot(
                ex, wb_ref[...], preferred_element_type=F32)
            return carry
        lax.fori_loop(0, nsub, bproj, 0)

    a_re = are_ref[...]
    a_im = aim_ref[...]

    def step(t, carry):
        r0 = pl.multiple_of(t * S5_FOLD, S5_FOLD)
        new = []
        for b in range(nb):
            x_re, x_im = carry[b]
            n_re = a_re * x_re - a_im * x_im + bu[b, pl.ds(r0, S5_FOLD), 0:S5_HALF]
            n_im = a_re * x_im + a_im * x_re + bu[b, pl.ds(r0, S5_FOLD), S5_HALF:2 * S5_HALF]
            bu[b, pl.ds(r0, S5_FOLD), 0:S5_HALF] = n_re
            bu[b, pl.ds(r0, S5_FOLD), S5_HALF:2 * S5_HALF] = n_im
            new.append((n_re, n_im))
        return tuple(new)

    init = tuple((st[b, :, 0:S5_HALF], st[b, :, S5_HALF:2 * S5_HALF]) for b in range(nb))
    fin = lax.fori_loop(0, lt, step, init, unroll=S5_UNROLL)
    for b in range(nb):
        st[b, :, 0:S5_HALF] = fin[b][0]
        st[b, :, S5_HALF:2 * S5_HALF] = fin[b][1]

    for b in range(nb):
        def cproj(s, carry, b=b):
            t0 = pl.multiple_of(s * S5_SUB, S5_SUB)
            xs = bu[b, pl.ds(pl.multiple_of(s * rows, rows), rows), :].astype(BF16)
            oc = jnp.dot(xs, wc_ref[...], preferred_element_type=F32)
            oc = jnp.where(fold_mask, oc, 0.0)
            y = jnp.sum(oc.reshape(S5_SUB, S5_FOLD, D_S5), axis=1)
            y = y + dsk_ref[...] * u_ref[b, pl.ds(t0, S5_SUB), :].astype(F32)
            g = _gelu_tanh(y)
            z = jnp.dot(g.astype(BF16), wglu_ref[...], preferred_element_type=F32)
            ys = g * jax.nn.sigmoid(z)
            ys = ys * lax.rsqrt(jnp.mean(ys * ys, axis=-1, keepdims=True) + RMS_EPS) * nw_ref[...]
            out_ref[b, pl.ds(t0, S5_SUB), :] = ys.astype(BF16)
            return carry
        lax.fori_loop(0, nsub, cproj, 0)


def _s5(u, wb, a_re, a_im, wc, dskip, wglu, norm_w, lt):
    nb, s, _ = u.shape
    blk = pl.BlockSpec((nb, lt, D_S5), lambda i: (0, i, 0))
    kern = functools.partial(_s5_kernel, nb=nb, lt=lt)
    return pl.pallas_call(
        kern,
        grid=(s // lt,),
        in_specs=[blk, _const_spec((D_S5, D_S5)), _const_spec((S5_FOLD, S5_HALF)), _const_spec((S5_FOLD, S5_HALF)),
                  _const_spec((D_S5, D_S5)), _const_spec((1, D_S5)), _const_spec((D_S5, D_S5)),
                  _const_spec((1, D_S5))],
        out_specs=blk,
        out_shape=jax.ShapeDtypeStruct((nb, s, D_S5), BF16),
        scratch_shapes=[pltpu.VMEM((nb, lt * S5_FOLD, D_S5), F32),
                        pltpu.VMEM((nb, S5_FOLD, D_S5), F32)],
        compiler_params=_params(("arbitrary",)),
        name="s5",
    )(u, wb, a_re, a_im, wc, dskip, wglu, norm_w)


def _s5_discretise(a_re, a_im, log_dt, b_re, b_im, c_re, c_im):
    dt = jnp.exp(log_dt)[:, None]
    mag = jnp.exp(a_re * dt)
    ab_re = mag * jnp.cos(a_im * dt)
    ab_im = mag * jnp.sin(a_im * dt)
    lam2 = a_re * a_re + a_im * a_im
    z_re = ab_re - 1.0
    f_re = (z_re * a_re + ab_im * a_im) / lam2
    f_im = (ab_im * a_re - z_re * a_im) / lam2
    bb_re = f_re[..., None] * b_re - f_im[..., None] * b_im
    bb_im = f_re[..., None] * b_im + f_im[..., None] * b_re

    gl = S5_GROUPS_PER_ROW
    a_re_t = ab_re.reshape(S5_FOLD, S5_HALF)
    a_im_t = ab_im.reshape(S5_FOLD, S5_HALF)

    def pack_b(bb):
        x = bb.reshape(S5_FOLD, gl, S5_STATE, S5_GROUP).transpose(0, 1, 3, 2)
        eye = jnp.eye(gl, dtype=F32)
        return jnp.einsum('rghp,gk->rghkp', x, eye).reshape(D_S5, S5_HALF)
    wb = jnp.concatenate([pack_b(bb_re), pack_b(bb_im)], axis=1)

    def pack_c(c):
        x = c.reshape(S5_FOLD, gl, S5_GROUP, S5_STATE)
        eye = jnp.eye(gl, dtype=F32)
        return jnp.einsum('rghp,gk->kprgh', x, eye).reshape(S5_HALF, D_S5)
    wc = jnp.concatenate([pack_c(c_re), -pack_c(c_im)], axis=0)
    return wb.astype(BF16), a_re_t, a_im_t, wc.astype(BF16)


def _outproj_kernel(x_ref, hm_ref, ys_ref, w1_ref, w2_ref, out_ref):
    out_ref[...] = (x_ref[...]
                    + jnp.dot(hm_ref[...], w1_ref[...], preferred_element_type=F32)
                    + jnp.dot(ys_ref[...], w2_ref[...], preferred_element_type=F32))


def _outproj(x2, hm, ys, w1, w2, tm):
    t = x2.shape[0]
    row = lambda n: pl.BlockSpec((tm, n), lambda i: (i, 0))
    return pl.pallas_call(
        _outproj_kernel,
        grid=(t // tm,),
        in_specs=[row(D_MODEL), row(D_MLSTM), row(D_S5), _const_spec(w1.shape), _const_spec(w2.shape)],
        out_specs=row(D_MODEL),
        out_shape=jax.ShapeDtypeStruct((t, D_MODEL), F32),
        compiler_params=_params(("arbitrary",)),
        name="outproj",
    )(x2, hm, ys, w1, w2)


FF_CHUNK = 256


def _ffn_kernel(x_ref, g_ref, wg_ref, wu_ref, wd_ref, out_ref, hn_s, *, d_ff):
    x = x_ref[...]
    hn_s[...] = _rmsnorm(x, g_ref[...]).astype(BF16)
    out_ref[...] = x
    for c in range(d_ff // FF_CHUNK):
        cs = slice(c * FF_CHUNK, (c + 1) * FF_CHUNK)
        hn = hn_s[...]
        a = jnp.dot(hn, wg_ref[:, cs], preferred_element_type=F32)
        b = jnp.dot(hn, wu_ref[:, cs], preferred_element_type=F32)
        h = (a * jax.nn.sigmoid(a) * b).astype(BF16)
        out_ref[...] += jnp.dot(h, wd_ref[cs, :], preferred_element_type=F32)


def _ffn(x2, g, wg, wu, wd, tm):
    t = x2.shape[0]
    d_ff = wg.shape[1]
    row = pl.BlockSpec((tm, D_MODEL), lambda i: (i, 0))
    return pl.pallas_call(
        functools.partial(_ffn_kernel, d_ff=d_ff),
        grid=(t // tm,),
        in_specs=[row, _const_spec((1, D_MODEL)), _const_spec(wg.shape), _const_spec(wu.shape),
                  _const_spec(wd.shape)],
        out_specs=row,
        out_shape=jax.ShapeDtypeStruct((t, D_MODEL), F32),
        scratch_shapes=[pltpu.VMEM((tm, D_MODEL), BF16)],
        compiler_params=_params(("arbitrary",)),
        name="ffn_dense",
    )(x2, g, wg, wu, wd)


def _router_kernel(x_ref, g_ref, wrt_ref, br_ref, hn_ref, comb_ref, pos_ref, cnt_ref, *, tb):
    hn = _rmsnorm(x_ref[...], g_ref[...])
    hn_ref[...] = hn.astype(BF16)
    logits = lax.dot_general(wrt_ref[...], hn, (((1,), (1,)), ((), ())),
                             preferred_element_type=F32, precision=HIGHEST) + br_ref[...]
    eidx = lax.broadcasted_iota(I32, logits.shape, 0)
    v1 = jnp.max(logits, axis=0, keepdims=True)
    i1 = jnp.min(jnp.where(logits == v1, eidx, N_EXPERTS), axis=0, keepdims=True)
    m1 = eidx == i1
    rest = jnp.where(m1, -jnp.inf, logits)
    v2 = jnp.max(rest, axis=0, keepdims=True)
    i2 = jnp.min(jnp.where(rest == v2, eidx, N_EXPERTS), axis=0, keepdims=True)
    m2 = eidx == i2
    e2 = jnp.exp(v2 - v1)
    g1 = 1.0 / (1.0 + e2)
    g2 = e2 / (1.0 + e2)
    comb_ref[...] = jnp.where(m1, g1, 0.0) + jnp.where(m2, g2, 0.0)
    sel = m1 | m2
    tr = lax.broadcasted_iota(I32, (tb, tb), 0)
    tc = lax.broadcasted_iota(I32, (tb, tb), 1)
    upper = (tr <= tc).astype(BF16)
    incl = jnp.dot(sel.astype(BF16), upper, preferred_element_type=F32)
    pos_ref[...] = jnp.where(sel, incl.astype(I32) - 1, -1)
    cnt_ref[0] = jnp.broadcast_to(incl[:, tb - 1:tb], (N_EXPERTS, LANES)).astype(I32)


def _router(x2, g, wrt, br, tb):
    t = x2.shape[0]
    nblk = t // tb
    return pl.pallas_call(
        functools.partial(_router_kernel, tb=tb),
        grid=(nblk,),
        in_specs=[pl.BlockSpec((tb, D_MODEL), lambda i: (i, 0)), _const_spec((1, D_MODEL)),
                  _const_spec((N_EXPERTS, D_MODEL)), _const_spec((N_EXPERTS, 1))],
        out_specs=[pl.BlockSpec((tb, D_MODEL), lambda i: (i, 0)),
                   pl.BlockSpec((N_EXPERTS, tb), lambda i: (0, i)),
                   pl.BlockSpec((N_EXPERTS, tb), lambda i: (0, i)),
                   pl.BlockSpec((1, N_EXPERTS, LANES), lambda i: (i, 0, 0))],
        out_shape=[jax.ShapeDtypeStruct((t, D_MODEL), BF16), jax.ShapeDtypeStruct((N_EXPERTS, t), F32),
                   jax.ShapeDtypeStruct((N_EXPERTS, t), I32),
                   jax.ShapeDtypeStruct((nblk, N_EXPERTS, LANES), I32)],
        compiler_params=_params(("arbitrary",)),
        name="router",
    )(x2, g, wrt, br)


SLOT_TILE = 128


def _moe_kernel(cnt_ref, x_ref, hn_ref, pos_ref, comb_ref, wg_ref, wu_ref, wd_ref, out_ref,
                xe, ye, *, tb, nf):
    i, e, f = pl.program_id(0), pl.program_id(1), pl.program_id(2)
    n = cnt_ref[i * N_EXPERTS + e]
    ntiles = (n + SLOT_TILE - 1) // SLOT_TILE
    pos_row = pos_ref[pl.ds(e, 1), :]
    slot_iota = lax.broadcasted_iota(I32, (SLOT_TILE, tb), 0)

    @pl.when((e == 0) & (f == 0))
    def _residual():
        out_ref[...] = x_ref[...]

    @pl.when(f == 0)
    def _gather():
        def body(j, carry):
            j0 = pl.multiple_of(j * SLOT_TILE, SLOT_TILE)
            onehot = (pos_row == slot_iota + j0).astype(BF16)
            xe[pl.ds(j0, SLOT_TILE), :] = jnp.dot(
                onehot, hn_ref[...], preferred_element_type=F32).astype(BF16)
            ye[pl.ds(j0, SLOT_TILE), :] = jnp.zeros((SLOT_TILE, D_MODEL), F32)
            return carry
        lax.fori_loop(0, ntiles, body, 0)

    def expert(j, carry):
        j0 = pl.multiple_of(j * SLOT_TILE, SLOT_TILE)
        xs = xe[pl.ds(j0, SLOT_TILE), :]
        a = jnp.dot(xs, wg_ref[0], preferred_element_type=F32)
        b = jnp.dot(xs, wu_ref[0], preferred_element_type=F32)
        h = (a * jax.nn.sigmoid(a) * b).astype(BF16)
        ye[pl.ds(j0, SLOT_TILE), :] += jnp.dot(h, wd_ref[0], preferred_element_type=F32)
        return carry
    lax.fori_loop(0, ntiles, expert, 0)

    @pl.when(f == nf - 1)
    def _scatter():
        comb_row = comb_ref[pl.ds(e, 1), :]

        def body(j, carry):
            j0 = pl.multiple_of(j * SLOT_TILE, SLOT_TILE)
            hit = pos_row == slot_iota + j0
            gate = jnp.sum(jnp.where(hit, comb_row, 0.0), axis=1, keepdims=True)
            ys = (ye[pl.ds(j0, SLOT_TILE), :] * gate).astype(BF16)
            out_ref[...] += lax.dot_general(hit.astype(BF16), ys, (((0,), (0,)), ((), ())),
                                            preferred_element_type=F32)
            return carry
        lax.fori_loop(0, ntiles, body, 0)


def _moe(x2, hn, pos, comb, cnt, wg, wu, wd, tb, nf):
    t = x2.shape[0]
    d_ff = wg.shape[2]
    fc = d_ff // nf
    tok = lambda n: pl.BlockSpec((tb, n), lambda i, e, f, c: (i, 0))
    rowinfo = pl.BlockSpec((N_EXPERTS, tb), lambda i, e, f, c: (0, i))
    grid_spec = pltpu.PrefetchScalarGridSpec(
        num_scalar_prefetch=1,
        grid=(t // tb, N_EXPERTS, nf),
        in_specs=[tok(D_MODEL), tok(D_MODEL), rowinfo, rowinfo,
                  pl.BlockSpec((1, D_MODEL, fc), lambda i, e, f, c: (e, 0, f)),
                  pl.BlockSpec((1, D_MODEL, fc), lambda i, e, f, c: (e, 0, f)),
                  pl.BlockSpec((1, fc, D_MODEL), lambda i, e, f, c: (e, f, 0))],
        out_specs=tok(D_MODEL),
        scratch_shapes=[pltpu.VMEM((tb, D_MODEL), BF16), pltpu.VMEM((tb, D_MODEL), F32)],
    )
    return pl.pallas_call(
        functools.partial(_moe_kernel, tb=tb, nf=nf),
        grid_spec=grid_spec,
        out_shape=jax.ShapeDtypeStruct((t, D_MODEL), F32),
        compiler_params=_params(("arbitrary", "arbitrary", "arbitrary")),
        name="moe_experts",
    )(cnt, x2, hn, pos, comb, wg, wu, wd)


def _final_norm_kernel(x_ref, g_ref, out_ref):
    out_ref[...] = _rmsnorm(x_ref[...], g_ref[...])


def _final_norm(x2, g, tm):
    t = x2.shape[0]
    row = pl.BlockSpec((tm, D_MODEL), lambda i: (i, 0))
    return pl.pallas_call(
        _final_norm_kernel,
        grid=(t // tm,),
        in_specs=[row, _const_spec((1, D_MODEL))],
        out_specs=row,
        out_shape=jax.ShapeDtypeStruct((t, D_MODEL), F32),
        compiler_params=_params(("arbitrary",)),
        name="final_norm",
    )(x2, g)


def _tile(n, pref):
    return pref if n % pref == 0 else n


def kernel(x, norm_mix, w_in, conv_qk, b_igate, b_fgate, norm_mlstm, s5_a_re, s5_a_im, s5_log_dt, s5_b_re, s5_b_im, s5_c_re, s5_c_im, s5_d, w_glu, norm_s5, w_out, norm_ffn, ffn_w_gate, ffn_w_up, ffn_w_down, moe_w_router, moe_b_router, moe_w_gate, moe_w_up, moe_w_down, norm_final):
    nb, s, d = x.shape
    assert d == D_MODEL
    t = nb * s
    depth = w_in.shape[0]
    tm = _tile(t, 512)
    lt = _tile(s, 512)
    chunk = _tile(lt, 128)
    tb = _tile(t, 1024)
    nf = 2

    x2 = x.reshape(t, d)
    c0, c1, c2, c3 = D_MLSTM, 2 * D_MLSTM, 3 * D_MLSTM, 4 * D_MLSTM
    cu = c3 + N_GATE_COLS
    for l in range(depth):
        w = w_in[l]
        wg = jnp.zeros((d, LANES), F32).at[:, :N_GATE_COLS].set(w[:, c3:cu]).astype(BF16)
        bg = jnp.zeros((1, LANES), F32).at[0, :HEADS].set(b_igate[l]).at[0, HEADS:N_GATE_COLS].set(b_fgate[l])
        qk, v, o, u, gates = _inproj(
            x2, norm_mix[l][None], w[:, :c1].astype(BF16), w[:, c1:c2].astype(BF16),
            w[:, c2:c3].astype(BF16), w[:, cu:].astype(BF16), wg, bg, tm)
        r3 = lambda a: a.reshape(nb, s, a.shape[-1])
        hm = _mlstm(r3(qk), r3(v), r3(o), r3(gates), conv_qk[l], norm_mlstm[l][None], lt, chunk)
        wb, a_re_t, a_im_t, wc = _s5_discretise(s5_a_re[l], s5_a_im[l], s5_log_dt[l], s5_b_re[l], s5_b_im[l],
                                                s5_c_re[l], s5_c_im[l])
        ys = _s5(r3(u), wb, a_re_t, a_im_t, wc, s5_d[l][None], w_glu[l].astype(BF16), norm_s5[l][None], lt)
        x2 = _outproj(x2, hm.reshape(t, D_MLSTM), ys.reshape(t, D_S5),
                      w_out[l, :D_MLSTM].astype(BF16), w_out[l, D_MLSTM:].astype(BF16), tm)
        j = l // 2
        if l % 2 == 0:
            x2 = _ffn(x2, norm_ffn[l][None], ffn_w_gate[j].astype(BF16), ffn_w_up[j].astype(BF16),
                      ffn_w_down[j].astype(BF16), tm)
        else:
            hn, comb, pos, cnt = _router(x2, norm_ffn[l][None], moe_w_router[j].T, moe_b_router[j][:, None], tb)
            x2 = _moe(x2, hn, pos, comb, cnt[:, :, 0].reshape(-1), moe_w_gate[j].astype(BF16),
                      moe_w_up[j].astype(BF16), moe_w_down[j].astype(BF16), tb, nf)
    out = _final_norm(x2, norm_final[None], tm)
    return out.reshape(nb, s, d)
```

```python
import functools

import jax
import jax.numpy as jnp
from jax import lax
from jax.experimental import pallas as pl
from jax.experimental.pallas import tpu as pltpu

F32, BF16, I32 = jnp.float32, jnp.bfloat16, jnp.int32
HIGHEST = lax.Precision.HIGHEST

RMS_EPS = 1e-6
D_MODEL = 1024
D_MLSTM = 512
D_S5 = 512
HEADS = 4
HEAD_DIM = 128
CONV_WIDTH = 4
S5_GROUP = 16
S5_GROUPS = 32
S5_STATE = 64
N_EXPERTS = 8
N_GATE_COLS = 2 * HEADS

LANES = 128
SUBLANES = 8
VMEM_LIMIT = 56 * 1024 * 1024

S5_FOLD = SUBLANES
S5_GROUPS_PER_ROW = S5_GROUPS // S5_FOLD
S5_HALF = S5_GROUPS_PER_ROW * S5_STATE
S5_CH_PER_ROW = S5_GROUPS_PER_ROW * S5_GROUP
S5_SLABS = D_S5 // LANES
S5_C_ROWS = 4


def _params(sem):
    return pltpu.CompilerParams(dimension_semantics=sem, vmem_limit_bytes=VMEM_LIMIT)


def _const_spec(shape):
    nd = len(shape)
    return pl.BlockSpec(shape, lambda *_: (0,) * nd, pipeline_mode=pl.Buffered(1))


def _rmsnorm(x, g):
    y = x * lax.rsqrt(jnp.mean(x * x, axis=-1, keepdims=True) + RMS_EPS)
    return y * g


def _log_sigmoid(x):
    return jnp.minimum(x, 0.0) - jnp.log1p(jnp.exp(-jnp.abs(x)))


def _inproj_kernel(x_ref, g_ref, wqk_ref, wv_ref, wo_ref, wu_ref, wg_ref, bg_ref,
                   qk_ref, v_ref, o_ref, u_ref, gate_ref):
    hn = _rmsnorm(x_ref[...], g_ref[...]).astype(BF16)
    qk_ref[...] = jnp.dot(hn, wqk_ref[...], preferred_element_type=F32).astype(BF16)
    v_ref[...] = jnp.dot(hn, wv_ref[...], preferred_element_type=F32).astype(BF16)
    o_ref[...] = jnp.dot(hn, wo_ref[...], preferred_element_type=F32).astype(BF16)
    u_ref[...] = jnp.dot(hn, wu_ref[...], preferred_element_type=F32).astype(BF16)
    gp = jnp.dot(hn, wg_ref[...], preferred_element_type=F32) + bg_ref[...]
    lane = lax.broadcasted_iota(I32, gp.shape, 1)
    gate_ref[...] = jnp.where(lane < HEADS, gp, _log_sigmoid(gp))


def _inproj(x2, g, wqk, wv, wo, wu, wg, bg, tm):
    t = x2.shape[0]
    row = lambda n: pl.BlockSpec((tm, n), lambda i: (i, 0))
    return pl.pallas_call(
        _inproj_kernel,
        grid=(t // tm,),
        in_specs=[row(D_MODEL), _const_spec((1, D_MODEL)), _const_spec(wqk.shape), _const_spec(wv.shape),
                  _const_spec(wo.shape), _const_spec(wu.shape), _const_spec(wg.shape), _const_spec((1, LANES))],
        out_specs=[row(2 * D_MLSTM), row(D_MLSTM), row(D_MLSTM), row(D_S5), row(LANES)],
        out_shape=[jax.ShapeDtypeStruct((t, 2 * D_MLSTM), BF16), jax.ShapeDtypeStruct((t, D_MLSTM), BF16),
                   jax.ShapeDtypeStruct((t, D_MLSTM), BF16), jax.ShapeDtypeStruct((t, D_S5), BF16),
                   jax.ShapeDtypeStruct((t, LANES), F32)],
        compiler_params=_params(("arbitrary",)),
        name="inproj",
    )(x2, g, wqk, wv, wo, wu, wg, bg)


CONV_ROWS = 32
HALO = SUBLANES


def _mlstm_kernel(qk_ref, v_ref, o_ref, g_ref, cw_ref, nw_ref, out_ref,
                  xbuf, qkact, cstate, mstate, *, nb, lt, chunk):
    i = pl.program_id(0)

    @pl.when(i == 0)
    def _init():
        xbuf[:, 0:HALO, :] = jnp.zeros((nb, HALO, 2 * D_MLSTM), F32)
        cstate[...] = jnp.zeros(cstate.shape, F32)
        mstate[...] = jnp.zeros(mstate.shape, F32)

    lane = lax.broadcasted_iota(I32, (1, 2 * D_MLSTM), 1)
    qk_scale = jnp.where(lane < D_MLSTM, 1.0, HEAD_DIM ** -0.5).astype(F32)
    for b in range(nb):
        xbuf[b, HALO:HALO + lt, :] = qk_ref[b].astype(F32)

        def conv_block(r, carry, b=b):
            r0 = pl.multiple_of(r * CONV_ROWS, CONV_ROWS)
            win = xbuf[b, pl.ds(r0, CONV_ROWS + HALO), :]
            acc = win[HALO:, :] * cw_ref[CONV_WIDTH - 1:CONV_WIDTH, :]
            for j in range(1, CONV_WIDTH):
                acc = acc + win[HALO - j:HALO - j + CONV_ROWS, :] * cw_ref[CONV_WIDTH - 1 - j:CONV_WIDTH - j, :]
            act = acc * jax.nn.sigmoid(acc) * qk_scale
            qkact[b, pl.ds(r0, CONV_ROWS), :] = act.astype(BF16)
            return carry

        lax.fori_loop(0, lt // CONV_ROWS, conv_block, 0)
        xbuf[b, 0:HALO, :] = xbuf[b, lt:lt + HALO, :]

    L = chunk
    rr = lax.broadcasted_iota(I32, (L, L), 0)
    cc = lax.broadcasted_iota(I32, (L, L), 1)
    causal = rr >= cc
    tri = causal.astype(F32)
    ones_col = (lax.broadcasted_iota(I32, (L, HEAD_DIM), 1) == 0).astype(BF16)

    def chunk_body(c, carry):
        r0 = pl.multiple_of(c * L, L)
        for b in range(nb):
            gates = g_ref[b, pl.ds(r0, L), :]
            csum = jnp.dot(tri, gates, preferred_element_type=F32, precision=HIGHEST)
            gates_t = gates.T
            csum_t = csum.T
            for h in range(HEADS):
                sl = slice(h * HEAD_DIM, (h + 1) * HEAD_DIM)
                slk = slice(D_MLSTM + h * HEAD_DIM, D_MLSTM + (h + 1) * HEAD_DIM)
                li_col = gates[:, h:h + 1]
                li_row = gates_t[h:h + 1, :]
                b_col = csum[:, HEADS + h:HEADS + h + 1]
                b_row = csum_t[HEADS + h:HEADS + h + 1, :]
                m_prev = mstate[b * HEADS + h, 0:1, 0:1]

                log_d = jnp.where(causal, b_col - b_row + li_row, -jnp.inf)
                m_inter = b_col + m_prev
                m_t = jnp.maximum(m_inter, jnp.max(log_d, axis=-1, keepdims=True))
                d = jnp.exp(log_d - m_t)
                inter = jnp.exp(m_inter - m_t)

                q = qkact[b, pl.ds(r0, L), sl]
                k = qkact[b, pl.ds(r0, L), slk]
                v = v_ref[b, pl.ds(r0, L), sl]
                vext = jnp.concatenate([v, ones_col], axis=1)
                s = lax.dot_general(q, k, (((1,), (1,)), ((), ())), preferred_element_type=F32)
                p = (s * d).astype(BF16)
                cext = cstate[b * HEADS + h]
                r = (jnp.dot(p, vext, preferred_element_type=F32)
                     + inter * jnp.dot(q, cext.astype(BF16), preferred_element_type=F32))
                num = r[:, :HEAD_DIM]
                den = r[:, HEAD_DIM:HEAD_DIM + 1]
                hh = num / jnp.maximum(jnp.abs(den), jnp.exp(-m_t))
                hh = hh * lax.rsqrt(jnp.mean(hh * hh, axis=-1, keepdims=True) + RMS_EPS)
                hh = hh * nw_ref[:, sl]
                og = jax.nn.sigmoid(o_ref[b, pl.ds(r0, L), sl].astype(F32))
                out_ref[b, pl.ds(r0, L), sl] = (og * hh).astype(BF16)

                b_end = b_col[L - 1:L, :]
                log_w = b_end - b_col + li_col
                m_new = jnp.maximum(b_end + m_prev, jnp.max(log_w, axis=0, keepdims=True))
                w = jnp.exp(log_w - m_new)
                decay = jnp.exp(b_end + m_prev - m_new)
                kw = (k.astype(F32) * w).astype(BF16)
                upd = lax.dot_general(kw, vext, (((0,), (0,)), ((), ())), preferred_element_type=F32)
                cstate[b * HEADS + h] = decay * cext + upd
                mstate[b * HEADS + h] = jnp.broadcast_to(m_new, (SUBLANES, LANES))
        return carry

    lax.fori_loop(0, lt // L, chunk_body, 0)


def _mlstm(qk, v, o, gates, conv_w, norm_w, lt, chunk):
    nb, s, _ = qk.shape
    blk = lambda n: pl.BlockSpec((nb, lt, n), lambda i: (0, i, 0))
    kern = functools.partial(_mlstm_kernel, nb=nb, lt=lt, chunk=chunk)
    return pl.pallas_call(
        kern,
        grid=(s // lt,),
        in_specs=[blk(2 * D_MLSTM), blk(D_MLSTM), blk(D_MLSTM), blk(LANES),
                  _const_spec((CONV_WIDTH, 2 * D_MLSTM)), _const_spec((1, D_MLSTM))],
        out_specs=blk(D_MLSTM),
        out_shape=jax.ShapeDtypeStruct((nb, s, D_MLSTM), BF16),
        scratch_shapes=[pltpu.VMEM((nb, lt + HALO, 2 * D_MLSTM), F32),
                        pltpu.VMEM((nb, lt, 2 * D_MLSTM), BF16),
                        pltpu.VMEM((nb * HEADS, HEAD_DIM, 2 * HEAD_DIM), F32),
                        pltpu.VMEM((nb * HEADS, SUBLANES, LANES), F32)],
        compiler_params=_params(("arbitrary",)),
        name="mlstm",
    )(qk, v, o, gates, conv_w, norm_w)


S5_SUB = 128
S5_UNROLL = 8


def _gelu_tanh(x):
    return 0.5 * x * (1.0 + jnp.tanh(0.7978845608028654 * (x + 0.044715 * (x * x * x))))


def _s5_kernel(u_ref, wb_ref, are_ref, aim_ref, wc_ref, dsk_ref, wglu_ref, nw_ref, out_ref,
               bu, st, *, nb, lt):
    i = pl.program_id(0)

    @pl.when(i == 0)
    def _init():
        st[...] = jnp.zeros(st.shape, F32)

    rows = S5_SUB * S5_FOLD
    nsub = lt // S5_SUB
    pairs = S5_FOLD // S5_SLABS

    for b in range(nb):
        def bproj(s, carry, b=b):
            t0 = pl.multiple_of(s * S5_SUB, S5_SUB)
            base = pl.multiple_of(s * rows, rows)
            for q in range(S5_SLABS):
                uq = u_ref[b, pl.ds(t0, S5_SUB), q * LANES:(q + 1) * LANES]
                for r in range(pairs * q, pairs * (q + 1)):
                    res = jnp.dot(uq, wb_ref[r], preferred_element_type=F32)
                    for k in range(S5_SLABS):
                        bu[b, k, pl.ds(base + r, S5_SUB, stride=S5_FOLD), :] = res[:, k * LANES:(k + 1) * LANES]
            return carry
        lax.fori_loop(0, nsub, bproj, 0)

    nh = S5_SLABS // 2
    a_re = [are_ref[:, j * LANES:(j + 1) * LANES] for j in range(nh)]
    a_im = [aim_ref[:, j * LANES:(j + 1) * LANES] for j in range(nh)]

    def step(t, carry):
        r0 = pl.multiple_of(t * S5_FOLD, S5_FOLD)
        new = []
        for b in range(nb):
            nxt = [None] * S5_SLABS
            for j in range(nh):
                x_re, x_im = carry[b][j], carry[b][nh + j]
                n_re = a_re[j] * x_re - a_im[j] * x_im + bu[b, j, pl.ds(r0, S5_FOLD), :]
                n_im = a_re[j] * x_im + a_im[j] * x_re + bu[b, nh + j, pl.ds(r0, S5_FOLD), :]
                bu[b, j, pl.ds(r0, S5_FOLD), :] = n_re
                bu[b, nh + j, pl.ds(r0, S5_FOLD), :] = n_im
                nxt[j], nxt[nh + j] = n_re, n_im
            new.append(tuple(nxt))
        return tuple(new)

    init = tuple(tuple(st[b, k] for k in range(S5_SLABS)) for b in range(nb))
    fin = lax.fori_loop(0, lt, step, init, unroll=S5_UNROLL)
    for b in range(nb):
        for k in range(S5_SLABS):
            st[b, k] = fin[b][k]

    for b in range(nb):
        def cproj(s, carry, b=b):
            t0 = pl.multiple_of(s * S5_SUB, S5_SUB)
            base = pl.multiple_of(s * rows, rows)
            cols = []
            for q in range(S5_FOLD // S5_C_ROWS):
                acc = None
                for r in range(S5_C_ROWS * q, S5_C_ROWS * (q + 1)):
                    xr = jnp.concatenate(
                        [bu[b, k, pl.ds(base + r, S5_SUB, stride=S5_FOLD), :] for k in range(S5_SLABS)],
                        axis=1).astype(BF16)
                    part = jnp.dot(xr, wc_ref[r], preferred_element_type=F32)
                    acc = part if acc is None else acc + part
                cols.append(acc)
            y = jnp.concatenate(cols, axis=1)
            y = y + dsk_ref[...] * u_ref[b, pl.ds(t0, S5_SUB), :].astype(F32)
            g = _gelu_tanh(y)
            z = jnp.dot(g.astype(BF16), wglu_ref[...], preferred_element_type=F32)
            ys = g * jax.nn.sigmoid(z)
            ys = ys * lax.rsqrt(jnp.mean(ys * ys, axis=-1, keepdims=True) + RMS_EPS) * nw_ref[...]
            out_ref[b, pl.ds(t0, S5_SUB), :] = ys.astype(BF16)
            return carry
        lax.fori_loop(0, nsub, cproj, 0)


def _s5(u, wb, a_re, a_im, wc, dskip, wglu, norm_w, lt):
    nb, s, _ = u.shape
    blk = pl.BlockSpec((nb, lt, D_S5), lambda i: (0, i, 0))
    kern = functools.partial(_s5_kernel, nb=nb, lt=lt)
    return pl.pallas_call(
        kern,
        grid=(s // lt,),
        in_specs=[blk, _const_spec(wb.shape), _const_spec((S5_FOLD, S5_HALF)), _const_spec((S5_FOLD, S5_HALF)),
                  _const_spec(wc.shape), _const_spec((1, D_S5)), _const_spec((D_S5, D_S5)),
                  _const_spec((1, D_S5))],
        out_specs=blk,
        out_shape=jax.ShapeDtypeStruct((nb, s, D_S5), BF16),
        scratch_shapes=[pltpu.VMEM((nb, S5_SLABS, lt * S5_FOLD, LANES), F32),
                        pltpu.VMEM((nb, S5_SLABS, S5_FOLD, LANES), F32)],
        compiler_params=_params(("arbitrary",)),
        name="s5",
    )(u, wb, a_re, a_im, wc, dskip, wglu, norm_w)


def _s5_discretise(a_re, a_im, log_dt, b_re, b_im, c_re, c_im):
    dt = jnp.exp(log_dt)[:, None]
    mag = jnp.exp(a_re * dt)
    ab_re = mag * jnp.cos(a_im * dt)
    ab_im = mag * jnp.sin(a_im * dt)
    lam2 = a_re * a_re + a_im * a_im
    z_re = ab_re - 1.0
    f_re = (z_re * a_re + ab_im * a_im) / lam2
    f_im = (ab_im * a_re - z_re * a_im) / lam2
    bb_re = f_re[..., None] * b_re - f_im[..., None] * b_im
    bb_im = f_re[..., None] * b_im + f_im[..., None] * b_re

    gl = S5_GROUPS_PER_ROW
    a_re_t = ab_re.reshape(S5_FOLD, S5_HALF)
    a_im_t = ab_im.reshape(S5_FOLD, S5_HALF)

    def pack_b(bb):
        x = bb.reshape(S5_FOLD, gl, S5_STATE, S5_GROUP).transpose(0, 1, 3, 2)
        eye = jnp.eye(gl, dtype=F32)
        return jnp.einsum('rghp,gk->rghkp', x, eye).reshape(D_S5, S5_HALF)
    wb = jnp.concatenate([pack_b(bb_re), pack_b(bb_im)], axis=1)

    def pack_c(c):
        x = c.reshape(S5_FOLD, gl, S5_GROUP, S5_STATE)
        eye = jnp.eye(gl, dtype=F32)
        return jnp.einsum('rghp,gk->kprgh', x, eye).reshape(S5_HALF, D_S5)
    wc = jnp.concatenate([pack_c(c_re), -pack_c(c_im)], axis=0)

    pairs = S5_FOLD // S5_SLABS
    place_b = jnp.eye(pairs, dtype=F32)[jnp.arange(S5_FOLD) % pairs]
    wb_r = wb.reshape(S5_FOLD, S5_CH_PER_ROW, D_S5)
    wb_r = jnp.einsum('rcl,rp->rpcl', wb_r, place_b).reshape(S5_FOLD, LANES, D_S5)
    place_c = jnp.eye(S5_C_ROWS, dtype=F32)[jnp.arange(S5_FOLD) % S5_C_ROWS]
    wc_r = wc.reshape(D_S5, S5_FOLD, S5_CH_PER_ROW).transpose(1, 0, 2)
    wc_r = jnp.einsum('rlc,rp->rlpc', wc_r, place_c).reshape(S5_FOLD, D_S5, S5_C_ROWS * S5_CH_PER_ROW)
    return wb_r.astype(BF16), a_re_t, a_im_t, wc_r.astype(BF16)


def _outproj_kernel(x_ref, hm_ref, ys_ref, w1_ref, w2_ref, out_ref):
    out_ref[...] = (x_ref[...]
                    + jnp.dot(hm_ref[...], w1_ref[...], preferred_element_type=F32)
                    + jnp.dot(ys_ref[...], w2_ref[...], preferred_element_type=F32))


def _outproj(x2, hm, ys, w1, w2, tm):
    t = x2.shape[0]
    row = lambda n: pl.BlockSpec((tm, n), lambda i: (i, 0))
    return pl.pallas_call(
        _outproj_kernel,
        grid=(t // tm,),
        in_specs=[row(D_MODEL), row(D_MLSTM), row(D_S5), _const_spec(w1.shape), _const_spec(w2.shape)],
        out_specs=row(D_MODEL),
        out_shape=jax.ShapeDtypeStruct((t, D_MODEL), F32),
        compiler_params=_params(("arbitrary",)),
        name="outproj",
    )(x2, hm, ys, w1, w2)


FF_CHUNK = 256


def _ffn_kernel(x_ref, g_ref, wg_ref, wu_ref, wd_ref, out_ref, hn_s, *, d_ff):
    x = x_ref[...]
    hn_s[...] = _rmsnorm(x, g_ref[...]).astype(BF16)
    out_ref[...] = x
    for c in range(d_ff // FF_CHUNK):
        cs = slice(c * FF_CHUNK, (c + 1) * FF_CHUNK)
        hn = hn_s[...]
        a = jnp.dot(hn, wg_ref[:, cs], preferred_element_type=F32)
        b = jnp.dot(hn, wu_ref[:, cs], preferred_element_type=F32)
        h = (a * jax.nn.sigmoid(a) * b).astype(BF16)
        out_ref[...] += jnp.dot(h, wd_ref[cs, :], preferred_element_type=F32)


def _ffn(x2, g, wg, wu, wd, tm):
    t = x2.shape[0]
    d_ff = wg.shape[1]
    row = pl.BlockSpec((tm, D_MODEL), lambda i: (i, 0))
    return pl.pallas_call(
        functools.partial(_ffn_kernel, d_ff=d_ff),
        grid=(t // tm,),
        in_specs=[row, _const_spec((1, D_MODEL)), _const_spec(wg.shape), _const_spec(wu.shape),
                  _const_spec(wd.shape)],
        out_specs=row,
        out_shape=jax.ShapeDtypeStruct((t, D_MODEL), F32),
        scratch_shapes=[pltpu.VMEM((tm, D_MODEL), BF16)],
        compiler_params=_params(("arbitrary",)),
        name="ffn_dense",
    )(x2, g, wg, wu, wd)


def _router_kernel(x_ref, g_ref, wrt_ref, br_ref, hn_ref, comb_ref, pos_ref, cnt_ref, tok_ref, *, tb):
    hn = _rmsnorm(x_ref[...], g_ref[...])
    hn_ref[...] = hn.astype(BF16)
    logits = lax.dot_general(wrt_ref[...], hn, (((1,), (1,)), ((), ())),
                             preferred_element_type=F32, precision=HIGHEST) + br_ref[...]
    eidx = lax.broadcasted_iota(I32, logits.shape, 0)
    v1 = jnp.max(logits, axis=0, keepdims=True)
    i1 = jnp.min(jnp.where(logits == v1, eidx, N_EXPERTS), axis=0, keepdims=True)
    m1 = eidx == i1
    rest = jnp.where(m1, -jnp.inf, logits)
    v2 = jnp.max(rest, axis=0, keepdims=True)
    i2 = jnp.min(jnp.where(rest == v2, eidx, N_EXPERTS), axis=0, keepdims=True)
    m2 = eidx == i2
    e2 = jnp.exp(v2 - v1)
    g1 = 1.0 / (1.0 + e2)
    g2 = e2 / (1.0 + e2)
    comb_ref[...] = jnp.where(m1, g1, 0.0) + jnp.where(m2, g2, 0.0)
    sel = m1 | m2
    tr = lax.broadcasted_iota(I32, (tb, tb), 0)
    tc = lax.broadcasted_iota(I32, (tb, tb), 1)
    upper = (tr <= tc).astype(BF16)
    incl = jnp.dot(sel.astype(BF16), upper, preferred_element_type=F32)
    pos = incl - 1.0
    pos_ref[...] = jnp.where(sel, pos.astype(I32), -1)
    cnt_ref[0] = jnp.broadcast_to(incl[:, tb - 1:tb], (N_EXPERTS, LANES)).astype(I32)
    p1 = jnp.sum(jnp.where(m1, pos, 0.0), axis=0, keepdims=True)
    p2 = jnp.sum(jnp.where(m2, pos, 0.0), axis=0, keepdims=True)
    ri = lax.broadcasted_iota(I32, (LANES, tb), 0)
    info = jnp.where(ri == 0, p1, jnp.where(ri == 1, p2, jnp.where(
        ri == 2, i1.astype(F32), jnp.where(ri == 3, i2.astype(F32), 0.0))))
    tok_ref[...] = info.T


def _router(x2, g, wrt, br, tb):
    t = x2.shape[0]
    nblk = t // tb
    return pl.pallas_call(
        functools.partial(_router_kernel, tb=tb),
        grid=(nblk,),
        in_specs=[pl.BlockSpec((tb, D_MODEL), lambda i: (i, 0)), _const_spec((1, D_MODEL)),
                  _const_spec((N_EXPERTS, D_MODEL)), _const_spec((N_EXPERTS, 1))],
        out_specs=[pl.BlockSpec((tb, D_MODEL), lambda i: (i, 0)),
                   pl.BlockSpec((N_EXPERTS, tb), lambda i: (0, i)),
                   pl.BlockSpec((N_EXPERTS, tb), lambda i: (0, i)),
                   pl.BlockSpec((1, N_EXPERTS, LANES), lambda i: (i, 0, 0)),
                   pl.BlockSpec((tb, LANES), lambda i: (i, 0))],
        out_shape=[jax.ShapeDtypeStruct((t, D_MODEL), BF16), jax.ShapeDtypeStruct((N_EXPERTS, t), F32),
                   jax.ShapeDtypeStruct((N_EXPERTS, t), I32),
                   jax.ShapeDtypeStruct((nblk, N_EXPERTS, LANES), I32),
                   jax.ShapeDtypeStruct((t, LANES), F32)],
        compiler_params=_params(("arbitrary",)),
        name="router",
    )(x2, g, wrt, br)


SLOT_TILE = 128
COMBINE_ROWS = 256


def _moe_kernel(cnt_ref, x_ref, hn_ref, pos_ref, comb_ref, tok_ref, wg_ref, wu_ref, wd_ref, out_ref,
                xe, ye, yall, *, tb, nf):
    i, e, f = pl.program_id(0), pl.program_id(1), pl.program_id(2)
    tiles_of = lambda k: (cnt_ref[i * N_EXPERTS + k] + SLOT_TILE - 1) // SLOT_TILE
    ntiles = tiles_of(e)
    seg_start = [jnp.int32(0)]
    for k in range(N_EXPERTS - 1):
        seg_start.append(seg_start[-1] + tiles_of(k) * SLOT_TILE)
    off_e = jnp.int32(0)
    for k in range(1, N_EXPERTS):
        off_e = jnp.where(e == k, seg_start[k], off_e)
    pos_row = pos_ref[pl.ds(e, 1), :]
    slot_iota = lax.broadcasted_iota(I32, (SLOT_TILE, tb), 0)

    @pl.when((e == 0) & (f == 0))
    def _clear():
        yall[...] = jnp.zeros(yall.shape, BF16)

    @pl.when(f == 0)
    def _gather():
        def body(j, carry):
            j0 = pl.multiple_of(j * SLOT_TILE, SLOT_TILE)
            onehot = (pos_row == slot_iota + j0).astype(BF16)
            xe[pl.ds(j0, SLOT_TILE), :] = jnp.dot(
                onehot, hn_ref[...], preferred_element_type=F32).astype(BF16)
            ye[pl.ds(j0, SLOT_TILE), :] = jnp.zeros((SLOT_TILE, D_MODEL), F32)
            return carry
        lax.fori_loop(0, ntiles, body, 0)

    def expert(j, carry):
        j0 = pl.multiple_of(j * SLOT_TILE, SLOT_TILE)
        xs = xe[pl.ds(j0, SLOT_TILE), :]
        a = jnp.dot(xs, wg_ref[0], preferred_element_type=F32)
        b = jnp.dot(xs, wu_ref[0], preferred_element_type=F32)
        h = (a * jax.nn.sigmoid(a) * b).astype(BF16)
        ye[pl.ds(j0, SLOT_TILE), :] += jnp.dot(h, wd_ref[0], preferred_element_type=F32)
        return carry
    lax.fori_loop(0, ntiles, expert, 0)

    @pl.when(f == nf - 1)
    def _scale():
        comb_row = comb_ref[pl.ds(e, 1), :]

        def body(j, carry):
            j0 = pl.multiple_of(j * SLOT_TILE, SLOT_TILE)
            hit = pos_row == slot_iota + j0
            gate = jnp.sum(jnp.where(hit, comb_row, 0.0), axis=1, keepdims=True)
            dst = pl.multiple_of(off_e + j0, SLOT_TILE)
            yall[pl.ds(dst, SLOT_TILE), :] = (ye[pl.ds(j0, SLOT_TILE), :] * gate).astype(BF16)
            return carry
        lax.fori_loop(0, ntiles, body, 0)

    @pl.when((e == N_EXPERTS - 1) & (f == nf - 1))
    def _combine():
        smax = yall.shape[0]
        lane_slot = lax.broadcasted_iota(I32, (COMBINE_ROWS, smax), 1).astype(F32)

        def body(r, carry):
            t0 = pl.multiple_of(r * COMBINE_ROWS, COMBINE_ROWS)
            info = tok_ref[pl.ds(t0, COMBINE_ROWS), :]
            s1, s2 = info[:, 0:1], info[:, 1:2]
            i1, i2 = info[:, 2:3], info[:, 3:4]
            for k in range(1, N_EXPERTS):
                start = seg_start[k].astype(F32)
                s1 = s1 + jnp.where(i1 == float(k), start, 0.0)
                s2 = s2 + jnp.where(i2 == float(k), start, 0.0)
            onehot = ((lane_slot == s1) | (lane_slot == s2)).astype(BF16)
            out_ref[pl.ds(t0, COMBINE_ROWS), :] = x_ref[pl.ds(t0, COMBINE_ROWS), :] + jnp.dot(
                onehot, yall[...], preferred_element_type=F32)
            return carry
        lax.fori_loop(0, tb // COMBINE_ROWS, body, 0)


def _moe(x2, hn, pos, comb, cnt, tokinfo, wg, wu, wd, tb, nf):
    t = x2.shape[0]
    d_ff = wg.shape[2]
    fc = d_ff // nf
    smax = 2 * tb + N_EXPERTS * SLOT_TILE
    single = dict(pipeline_mode=pl.Buffered(1))
    tok = lambda n: pl.BlockSpec((tb, n), lambda i, e, f, c: (i, 0), **single)
    rowinfo = pl.BlockSpec((N_EXPERTS, tb), lambda i, e, f, c: (0, i))
    grid_spec = pltpu.PrefetchScalarGridSpec(
        num_scalar_prefetch=1,
        grid=(t // tb, N_EXPERTS, nf),
        in_specs=[tok(D_MODEL), tok(D_MODEL), rowinfo, rowinfo, tok(LANES),
                  pl.BlockSpec((1, D_MODEL, fc), lambda i, e, f, c: (e, 0, f)),
                  pl.BlockSpec((1, D_MODEL, fc), lambda i, e, f, c: (e, 0, f)),
                  pl.BlockSpec((1, fc, D_MODEL), lambda i, e, f, c: (e, f, 0))],
        out_specs=pl.BlockSpec((tb, D_MODEL), lambda i, e, f, c: (i, 0)),
        scratch_shapes=[pltpu.VMEM((tb, D_MODEL), BF16), pltpu.VMEM((tb, D_MODEL), F32),
                        pltpu.VMEM((smax, D_MODEL), BF16)],
    )
    return pl.pallas_call(
        functools.partial(_moe_kernel, tb=tb, nf=nf),
        grid_spec=grid_spec,
        out_shape=jax.ShapeDtypeStruct((t, D_MODEL), F32),
        compiler_params=_params(("arbitrary", "arbitrary", "arbitrary")),
        name="moe_experts",
    )(cnt, x2, hn, pos, comb, tokinfo, wg, wu, wd)


def _final_norm_kernel(x_ref, g_ref, out_ref):
    out_ref[...] = _rmsnorm(x_ref[...], g_ref[...])


def _final_norm(x2, g, tm):
    t = x2.shape[0]
    row = pl.BlockSpec((tm, D_MODEL), lambda i: (i, 0))
    return pl.pallas_call(
        _final_norm_kernel,
        grid=(t // tm,),
        in_specs=[row, _const_spec((1, D_MODEL))],
        out_specs=row,
        out_shape=jax.ShapeDtypeStruct((t, D_MODEL), F32),
        compiler_params=_params(("arbitrary",)),
        name="final_norm",
    )(x2, g)


def _tile(n, pref):
    return pref if n % pref == 0 else n


def kernel(x, norm_mix, w_in, conv_qk, b_igate, b_fgate, norm_mlstm, s5_a_re, s5_a_im, s5_log_dt, s5_b_re, s5_b_im, s5_c_re, s5_c_im, s5_d, w_glu, norm_s5, w_out, norm_ffn, ffn_w_gate, ffn_w_up, ffn_w_down, moe_w_router, moe_b_router, moe_w_gate, moe_w_up, moe_w_down, norm_final):
    nb, s, d = x.shape
    assert d == D_MODEL
    t = nb * s
    depth = w_in.shape[0]
    tm = _tile(t, 512)
    lt = _tile(s, 512)
    chunk = _tile(lt, 128)
    tb = _tile(t, 1024)
    nf = 2

    x2 = x.reshape(t, d)
    c0, c1, c2, c3 = D_MLSTM, 2 * D_MLSTM, 3 * D_MLSTM, 4 * D_MLSTM
    cu = c3 + N_GATE_COLS
    for l in range(depth):
        w = w_in[l]
        wg = jnp.zeros((d, LANES), F32).at[:, :N_GATE_COLS].set(w[:, c3:cu]).astype(BF16)
        bg = jnp.zeros((1, LANES), F32).at[0, :HEADS].set(b_igate[l]).at[0, HEADS:N_GATE_COLS].set(b_fgate[l])
        qk, v, o, u, gates = _inproj(
            x2, norm_mix[l][None], w[:, :c1].astype(BF16), w[:, c1:c2].astype(BF16),
            w[:, c2:c3].astype(BF16), w[:, cu:].astype(BF16), wg, bg, tm)
        r3 = lambda a: a.reshape(nb, s, a.shape[-1])
        hm = _mlstm(r3(qk), r3(v), r3(o), r3(gates), conv_qk[l], norm_mlstm[l][None], lt, chunk)
        wb, a_re_t, a_im_t, wc = _s5_discretise(s5_a_re[l], s5_a_im[l], s5_log_dt[l], s5_b_re[l], s5_b_im[l],
                                                s5_c_re[l], s5_c_im[l])
        ys = _s5(r3(u), wb, a_re_t, a_im_t, wc, s5_d[l][None], w_glu[l].astype(BF16), norm_s5[l][None], lt)
        x2 = _outproj(x2, hm.reshape(t, D_MLSTM), ys.reshape(t, D_S5),
                      w_out[l, :D_MLSTM].astype(BF16), w_out[l, D_MLSTM:].astype(BF16), tm)
        j = l // 2
        if l % 2 == 0:
            x2 = _ffn(x2, norm_ffn[l][None], ffn_w_gate[j].astype(BF16), ffn_w_up[j].astype(BF16),
                      ffn_w_down[j].astype(BF16), tm)
        else:
            hn, comb, pos, cnt, tokinfo = _router(x2, norm_ffn[l][None], moe_w_router[j].T,
                                                  moe_b_router[j][:, None], tb)
            x2 = _moe(x2, hn, pos, comb, cnt[:, :, 0].reshape(-1), tokinfo, moe_w_gate[j].astype(BF16),
                      moe_w_up[j].astype(BF16), moe_w_down[j].astype(BF16), tb, nf)
    out = _final_norm(x2, norm_final[None], tm)
    return out.reshape(nb, s, d)
```

```python
import functools

import jax
import jax.numpy as jnp
from jax import lax
from jax.experimental import pallas as pl
from jax.experimental.pallas import tpu as pltpu

F32, BF16, I32 = jnp.float32, jnp.bfloat16, jnp.int32
HIGHEST = lax.Precision.HIGHEST

RMS_EPS = 1e-6
D_MODEL = 1024
D_MLSTM = 512
D_S5 = 512
HEADS = 4
HEAD_DIM = 128
CONV_WIDTH = 4
S5_GROUP = 16
S5_GROUPS = 32
S5_STATE = 64
N_EXPERTS = 8
N_GATE_COLS = 2 * HEADS

LANES = 128
SUBLANES = 8
VMEM_LIMIT = 56 * 1024 * 1024

S5_FOLD = SUBLANES
S5_GROUPS_PER_ROW = S5_GROUPS // S5_FOLD
S5_HALF = S5_GROUPS_PER_ROW * S5_STATE
S5_CH_PER_ROW = S5_GROUPS_PER_ROW * S5_GROUP
S5_SLABS = D_S5 // LANES
S5_C_ROWS = 4


def _params(sem):
    return pltpu.CompilerParams(dimension_semantics=sem, vmem_limit_bytes=VMEM_LIMIT)


def _const_spec(shape):
    nd = len(shape)
    return pl.BlockSpec(shape, lambda *_: (0,) * nd, pipeline_mode=pl.Buffered(1))


def _rmsnorm(x, g):
    y = x * lax.rsqrt(jnp.mean(x * x, axis=-1, keepdims=True) + RMS_EPS)
    return y * g


def _log_sigmoid(x):
    return jnp.minimum(x, 0.0) - jnp.log1p(jnp.exp(-jnp.abs(x)))


def _inproj_kernel(x_ref, g_ref, wqk_ref, wv_ref, wo_ref, wu_ref, wg_ref, bg_ref,
                   qk_ref, v_ref, o_ref, u_ref, gate_ref):
    hn = _rmsnorm(x_ref[...], g_ref[...]).astype(BF16)
    qk_ref[...] = jnp.dot(hn, wqk_ref[...], preferred_element_type=F32).astype(BF16)
    v_ref[...] = jnp.dot(hn, wv_ref[...], preferred_element_type=F32).astype(BF16)
    o_ref[...] = jnp.dot(hn, wo_ref[...], preferred_element_type=F32).astype(BF16)
    u_ref[...] = jnp.dot(hn, wu_ref[...], preferred_element_type=F32).astype(BF16)
    gp = jnp.dot(hn, wg_ref[...], preferred_element_type=F32) + bg_ref[...]
    lane = lax.broadcasted_iota(I32, gp.shape, 1)
    gate_ref[...] = jnp.where(lane < HEADS, gp, _log_sigmoid(gp))


def _inproj(x2, g, wqk, wv, wo, wu, wg, bg, tm):
    t = x2.shape[0]
    row = lambda n: pl.BlockSpec((tm, n), lambda i: (i, 0))
    return pl.pallas_call(
        _inproj_kernel,
        grid=(t // tm,),
        in_specs=[row(D_MODEL), _const_spec((1, D_MODEL)), _const_spec(wqk.shape), _const_spec(wv.shape),
                  _const_spec(wo.shape), _const_spec(wu.shape), _const_spec(wg.shape), _const_spec((1, LANES))],
        out_specs=[row(2 * D_MLSTM), row(D_MLSTM), row(D_MLSTM), row(D_S5), row(LANES)],
        out_shape=[jax.ShapeDtypeStruct((t, 2 * D_MLSTM), BF16), jax.ShapeDtypeStruct((t, D_MLSTM), BF16),
                   jax.ShapeDtypeStruct((t, D_MLSTM), BF16), jax.ShapeDtypeStruct((t, D_S5), BF16),
                   jax.ShapeDtypeStruct((t, LANES), F32)],
        compiler_params=_params(("arbitrary",)),
        name="inproj",
    )(x2, g, wqk, wv, wo, wu, wg, bg)


CONV_ROWS = 32
HALO = SUBLANES


def _mlstm_kernel(qk_ref, v_ref, o_ref, g_ref, cw_ref, nw_ref, out_ref,
                  xbuf, qkact, cstate, mstate, *, nb, lt, chunk):
    i = pl.program_id(0)

    @pl.when(i == 0)
    def _init():
        xbuf[:, 0:HALO, :] = jnp.zeros((nb, HALO, 2 * D_MLSTM), F32)
        cstate[...] = jnp.zeros(cstate.shape, F32)
        mstate[...] = jnp.zeros(mstate.shape, F32)

    lane = lax.broadcasted_iota(I32, (1, 2 * D_MLSTM), 1)
    qk_scale = jnp.where(lane < D_MLSTM, 1.0, HEAD_DIM ** -0.5).astype(F32)
    for b in range(nb):
        xbuf[b, HALO:HALO + lt, :] = qk_ref[b].astype(F32)

        def conv_block(r, carry, b=b):
            r0 = pl.multiple_of(r * CONV_ROWS, CONV_ROWS)
            win = xbuf[b, pl.ds(r0, CONV_ROWS + HALO), :]
            acc = win[HALO:, :] * cw_ref[CONV_WIDTH - 1:CONV_WIDTH, :]
            for j in range(1, CONV_WIDTH):
                acc = acc + win[HALO - j:HALO - j + CONV_ROWS, :] * cw_ref[CONV_WIDTH - 1 - j:CONV_WIDTH - j, :]
            act = acc * jax.nn.sigmoid(acc) * qk_scale
            qkact[b, pl.ds(r0, CONV_ROWS), :] = act.astype(BF16)
            return carry

        lax.fori_loop(0, lt // CONV_ROWS, conv_block, 0)
        xbuf[b, 0:HALO, :] = xbuf[b, lt:lt + HALO, :]

    L = chunk
    rr = lax.broadcasted_iota(I32, (L, L), 0)
    cc = lax.broadcasted_iota(I32, (L, L), 1)
    causal = rr >= cc
    tri = causal.astype(F32)
    ones_col = (lax.broadcasted_iota(I32, (L, HEAD_DIM), 1) == 0).astype(BF16)

    def chunk_body(c, carry):
        r0 = pl.multiple_of(c * L, L)
        for b in range(nb):
            gates = g_ref[b, pl.ds(r0, L), :]
            csum = jnp.dot(tri, gates, preferred_element_type=F32, precision=HIGHEST)
            gates_t = gates.T
            csum_t = csum.T
            for h in range(HEADS):
                sl = slice(h * HEAD_DIM, (h + 1) * HEAD_DIM)
                slk = slice(D_MLSTM + h * HEAD_DIM, D_MLSTM + (h + 1) * HEAD_DIM)
                li_col = gates[:, h:h + 1]
                li_row = gates_t[h:h + 1, :]
                b_col = csum[:, HEADS + h:HEADS + h + 1]
                b_row = csum_t[HEADS + h:HEADS + h + 1, :]
                m_prev = mstate[b * HEADS + h, 0:1, 0:1]

                log_d = jnp.where(causal, b_col - b_row + li_row, -jnp.inf)
                m_inter = b_col + m_prev
                m_t = jnp.maximum(m_inter, jnp.max(log_d, axis=-1, keepdims=True))
                d = jnp.exp(log_d - m_t)
                inter = jnp.exp(m_inter - m_t)

                q = qkact[b, pl.ds(r0, L), sl]
                k = qkact[b, pl.ds(r0, L), slk]
                v = v_ref[b, pl.ds(r0, L), sl]
                vext = jnp.concatenate([v, ones_col], axis=1)
                s = lax.dot_general(q, k, (((1,), (1,)), ((), ())), preferred_element_type=F32)
                p = (s * d).astype(BF16)
                cext = cstate[b * HEADS + h]
                r = (jnp.dot(p, vext, preferred_element_type=F32)
                     + inter * jnp.dot(q, cext.astype(BF16), preferred_element_type=F32))
                num = r[:, :HEAD_DIM]
                den = r[:, HEAD_DIM:HEAD_DIM + 1]
                hh = num / jnp.maximum(jnp.abs(den), jnp.exp(-m_t))
                hh = hh * lax.rsqrt(jnp.mean(hh * hh, axis=-1, keepdims=True) + RMS_EPS)
                hh = hh * nw_ref[:, sl]
                og = jax.nn.sigmoid(o_ref[b, pl.ds(r0, L), sl].astype(F32))
                out_ref[b, pl.ds(r0, L), sl] = (og * hh).astype(BF16)

                b_end = b_col[L - 1:L, :]
                log_w = b_end - b_col + li_col
                m_new = jnp.maximum(b_end + m_prev, jnp.max(log_w, axis=0, keepdims=True))
                w = jnp.exp(log_w - m_new)
                decay = jnp.exp(b_end + m_prev - m_new)
                kw = (k.astype(F32) * w).astype(BF16)
                upd = lax.dot_general(kw, vext, (((0,), (0,)), ((), ())), preferred_element_type=F32)
                cstate[b * HEADS + h] = decay * cext + upd
                mstate[b * HEADS + h] = jnp.broadcast_to(m_new, (SUBLANES, LANES))
        return carry

    lax.fori_loop(0, lt // L, chunk_body, 0)


def _mlstm(qk, v, o, gates, conv_w, norm_w, lt, chunk):
    nb, s, _ = qk.shape
    blk = lambda n: pl.BlockSpec((nb, lt, n), lambda i: (0, i, 0))
    kern = functools.partial(_mlstm_kernel, nb=nb, lt=lt, chunk=chunk)
    return pl.pallas_call(
        kern,
        grid=(s // lt,),
        in_specs=[blk(2 * D_MLSTM), blk(D_MLSTM), blk(D_MLSTM), blk(LANES),
                  _const_spec((CONV_WIDTH, 2 * D_MLSTM)), _const_spec((1, D_MLSTM))],
        out_specs=blk(D_MLSTM),
        out_shape=jax.ShapeDtypeStruct((nb, s, D_MLSTM), BF16),
        scratch_shapes=[pltpu.VMEM((nb, lt + HALO, 2 * D_MLSTM), F32),
                        pltpu.VMEM((nb, lt, 2 * D_MLSTM), BF16),
                        pltpu.VMEM((nb * HEADS, HEAD_DIM, 2 * HEAD_DIM), F32),
                        pltpu.VMEM((nb * HEADS, SUBLANES, LANES), F32)],
        compiler_params=_params(("arbitrary",)),
        name="mlstm",
    )(qk, v, o, gates, conv_w, norm_w)


S5_SUB = 128
S5_UNROLL = 8


def _gelu_tanh(x):
    return 0.5 * x * (1.0 + jnp.tanh(0.7978845608028654 * (x + 0.044715 * (x * x * x))))


def _s5_kernel(u_ref, wb_ref, are_ref, aim_ref, wc_ref, dsk_ref, wglu_ref, nw_ref, out_ref,
               bu, st, *, nb, lt):
    i = pl.program_id(0)

    @pl.when(i == 0)
    def _init():
        st[...] = jnp.zeros(st.shape, F32)

    rows = S5_SUB * S5_FOLD
    nsub = lt // S5_SUB
    pairs = S5_FOLD // S5_SLABS

    for b in range(nb):
        def bproj(s, carry, b=b):
            t0 = pl.multiple_of(s * S5_SUB, S5_SUB)
            base = pl.multiple_of(s * rows, rows)
            for q in range(S5_SLABS):
                uq = u_ref[b, pl.ds(t0, S5_SUB), q * LANES:(q + 1) * LANES]
                for r in range(pairs * q, pairs * (q + 1)):
                    res = jnp.dot(uq, wb_ref[r], preferred_element_type=F32)
                    for k in range(S5_SLABS):
                        bu[b, k, pl.ds(base + r, S5_SUB, stride=S5_FOLD), :] = res[:, k * LANES:(k + 1) * LANES]
            return carry
        lax.fori_loop(0, nsub, bproj, 0)

    nh = S5_SLABS // 2
    a_re = [are_ref[:, j * LANES:(j + 1) * LANES] for j in range(nh)]
    a_im = [aim_ref[:, j * LANES:(j + 1) * LANES] for j in range(nh)]

    def step(t, carry):
        r0 = pl.multiple_of(t * S5_FOLD, S5_FOLD)
        new = []
        for b in range(nb):
            nxt = [None] * S5_SLABS
            for j in range(nh):
                x_re, x_im = carry[b][j], carry[b][nh + j]
                n_re = a_re[j] * x_re - a_im[j] * x_im + bu[b, j, pl.ds(r0, S5_FOLD), :]
                n_im = a_re[j] * x_im + a_im[j] * x_re + bu[b, nh + j, pl.ds(r0, S5_FOLD), :]
                bu[b, j, pl.ds(r0, S5_FOLD), :] = n_re
                bu[b, nh + j, pl.ds(r0, S5_FOLD), :] = n_im
                nxt[j], nxt[nh + j] = n_re, n_im
            new.append(tuple(nxt))
        return tuple(new)

    init = tuple(tuple(st[b, k] for k in range(S5_SLABS)) for b in range(nb))
    fin = lax.fori_loop(0, lt, step, init, unroll=S5_UNROLL)
    for b in range(nb):
        for k in range(S5_SLABS):
            st[b, k] = fin[b][k]

    for b in range(nb):
        def cproj(s, carry, b=b):
            t0 = pl.multiple_of(s * S5_SUB, S5_SUB)
            base = pl.multiple_of(s * rows, rows)
            cols = []
            for q in range(S5_FOLD // S5_C_ROWS):
                acc = None
                for r in range(S5_C_ROWS * q, S5_C_ROWS * (q + 1)):
                    xr = jnp.concatenate(
                        [bu[b, k, pl.ds(base + r, S5_SUB, stride=S5_FOLD), :] for k in range(S5_SLABS)],
                        axis=1).astype(BF16)
                    part = jnp.dot(xr, wc_ref[r], preferred_element_type=F32)
                    acc = part if acc is None else acc + part
                cols.append(acc)
            y = jnp.concatenate(cols, axis=1)
            y = y + dsk_ref[...] * u_ref[b, pl.ds(t0, S5_SUB), :].astype(F32)
            g = _gelu_tanh(y)
            z = jnp.dot(g.astype(BF16), wglu_ref[...], preferred_element_type=F32)
            ys = g * jax.nn.sigmoid(z)
            ys = ys * lax.rsqrt(jnp.mean(ys * ys, axis=-1, keepdims=True) + RMS_EPS) * nw_ref[...]
            out_ref[b, pl.ds(t0, S5_SUB), :] = ys.astype(BF16)
            return carry
        lax.fori_loop(0, nsub, cproj, 0)


def _s5(u, wb, a_re, a_im, wc, dskip, wglu, norm_w, lt):
    nb, s, _ = u.shape
    blk = pl.BlockSpec((nb, lt, D_S5), lambda i: (0, i, 0))
    kern = functools.partial(_s5_kernel, nb=nb, lt=lt)
    return pl.pallas_call(
        kern,
        grid=(s // lt,),
        in_specs=[blk, _const_spec(wb.shape), _const_spec((S5_FOLD, S5_HALF)), _const_spec((S5_FOLD, S5_HALF)),
                  _const_spec(wc.shape), _const_spec((1, D_S5)), _const_spec((D_S5, D_S5)),
                  _const_spec((1, D_S5))],
        out_specs=blk,
        out_shape=jax.ShapeDtypeStruct((nb, s, D_S5), BF16),
        scratch_shapes=[pltpu.VMEM((nb, S5_SLABS, lt * S5_FOLD, LANES), F32),
                        pltpu.VMEM((nb, S5_SLABS, S5_FOLD, LANES), F32)],
        compiler_params=_params(("arbitrary",)),
        name="s5",
    )(u, wb, a_re, a_im, wc, dskip, wglu, norm_w)


def _s5_discretise(a_re, a_im, log_dt, b_re, b_im, c_re, c_im):
    dt = jnp.exp(log_dt)[:, None]
    mag = jnp.exp(a_re * dt)
    ab_re = mag * jnp.cos(a_im * dt)
    ab_im = mag * jnp.sin(a_im * dt)
    lam2 = a_re * a_re + a_im * a_im
    z_re = ab_re - 1.0
    f_re = (z_re * a_re + ab_im * a_im) / lam2
    f_im = (ab_im * a_re - z_re * a_im) / lam2
    bb_re = f_re[..., None] * b_re - f_im[..., None] * b_im
    bb_im = f_re[..., None] * b_im + f_im[..., None] * b_re

    gl = S5_GROUPS_PER_ROW
    a_re_t = ab_re.reshape(S5_FOLD, S5_HALF)
    a_im_t = ab_im.reshape(S5_FOLD, S5_HALF)

    def pack_b(bb):
        x = bb.reshape(S5_FOLD, gl, S5_STATE, S5_GROUP).transpose(0, 1, 3, 2)
        eye = jnp.eye(gl, dtype=F32)
        return jnp.einsum('rghp,gk->rghkp', x, eye).reshape(D_S5, S5_HALF)
    wb = jnp.concatenate([pack_b(bb_re), pack_b(bb_im)], axis=1)

    def pack_c(c):
        x = c.reshape(S5_FOLD, gl, S5_GROUP, S5_STATE)
        eye = jnp.eye(gl, dtype=F32)
        return jnp.einsum('rghp,gk->kprgh', x, eye).reshape(S5_HALF, D_S5)
    wc = jnp.concatenate([pack_c(c_re), -pack_c(c_im)], axis=0)

    pairs = S5_FOLD // S5_SLABS
    place_b = jnp.eye(pairs, dtype=F32)[jnp.arange(S5_FOLD) % pairs]
    wb_r = wb.reshape(S5_FOLD, S5_CH_PER_ROW, D_S5)
    wb_r = jnp.einsum('rcl,rp->rpcl', wb_r, place_b).reshape(S5_FOLD, LANES, D_S5)
    place_c = jnp.eye(S5_C_ROWS, dtype=F32)[jnp.arange(S5_FOLD) % S5_C_ROWS]
    wc_r = wc.reshape(D_S5, S5_FOLD, S5_CH_PER_ROW).transpose(1, 0, 2)
    wc_r = jnp.einsum('rlc,rp->rlpc', wc_r, place_c).reshape(S5_FOLD, D_S5, S5_C_ROWS * S5_CH_PER_ROW)
    return wb_r.astype(BF16), a_re_t, a_im_t, wc_r.astype(BF16)


def _outproj_kernel(x_ref, hm_ref, ys_ref, w1_ref, w2_ref, out_ref):
    out_ref[...] = (x_ref[...]
                    + jnp.dot(hm_ref[...], w1_ref[...], preferred_element_type=F32)
                    + jnp.dot(ys_ref[...], w2_ref[...], preferred_element_type=F32))


def _outproj(x2, hm, ys, w1, w2, tm):
    t = x2.shape[0]
    row = lambda n: pl.BlockSpec((tm, n), lambda i: (i, 0))
    return pl.pallas_call(
        _outproj_kernel,
        grid=(t // tm,),
        in_specs=[row(D_MODEL), row(D_MLSTM), row(D_S5), _const_spec(w1.shape), _const_spec(w2.shape)],
        out_specs=row(D_MODEL),
        out_shape=jax.ShapeDtypeStruct((t, D_MODEL), F32),
        compiler_params=_params(("arbitrary",)),
        name="outproj",
    )(x2, hm, ys, w1, w2)


FF_CHUNK = 256


def _ffn_kernel(x_ref, g_ref, wg_ref, wu_ref, wd_ref, out_ref, hn_s, *, d_ff):
    x = x_ref[...]
    hn_s[...] = _rmsnorm(x, g_ref[...]).astype(BF16)
    out_ref[...] = x
    for c in range(d_ff // FF_CHUNK):
        cs = slice(c * FF_CHUNK, (c + 1) * FF_CHUNK)
        hn = hn_s[...]
        a = jnp.dot(hn, wg_ref[:, cs], preferred_element_type=F32)
        b = jnp.dot(hn, wu_ref[:, cs], preferred_element_type=F32)
        h = (a * jax.nn.sigmoid(a) * b).astype(BF16)
        out_ref[...] += jnp.dot(h, wd_ref[cs, :], preferred_element_type=F32)


def _ffn(x2, g, wg, wu, wd, tm):
    t = x2.shape[0]
    d_ff = wg.shape[1]
    row = pl.BlockSpec((tm, D_MODEL), lambda i: (i, 0))
    return pl.pallas_call(
        functools.partial(_ffn_kernel, d_ff=d_ff),
        grid=(t // tm,),
        in_specs=[row, _const_spec((1, D_MODEL)), _const_spec(wg.shape), _const_spec(wu.shape),
                  _const_spec(wd.shape)],
        out_specs=row,
        out_shape=jax.ShapeDtypeStruct((t, D_MODEL), F32),
        scratch_shapes=[pltpu.VMEM((tm, D_MODEL), BF16)],
        compiler_params=_params(("arbitrary",)),
        name="ffn_dense",
    )(x2, g, wg, wu, wd)


def _router_kernel(x_ref, g_ref, wrt_ref, br_ref, hn_ref, comb_ref, pos_ref, cnt_ref, tok_ref, *, tb):
    hn = _rmsnorm(x_ref[...], g_ref[...])
    hn_ref[...] = hn.astype(BF16)
    logits = lax.dot_general(wrt_ref[...], hn, (((1,), (1,)), ((), ())),
                             preferred_element_type=F32, precision=HIGHEST) + br_ref[...]
    eidx = lax.broadcasted_iota(I32, logits.shape, 0)
    v1 = jnp.max(logits, axis=0, keepdims=True)
    i1 = jnp.min(jnp.where(logits == v1, eidx, N_EXPERTS), axis=0, keepdims=True)
    m1 = eidx == i1
    rest = jnp.where(m1, -jnp.inf, logits)
    v2 = jnp.max(rest, axis=0, keepdims=True)
    i2 = jnp.min(jnp.where(rest == v2, eidx, N_EXPERTS), axis=0, keepdims=True)
    m2 = eidx == i2
    e2 = jnp.exp(v2 - v1)
    g1 = 1.0 / (1.0 + e2)
    g2 = e2 / (1.0 + e2)
    comb_ref[...] = jnp.where(m1, g1, 0.0) + jnp.where(m2, g2, 0.0)
    sel = m1 | m2
    tr = lax.broadcasted_iota(I32, (tb, tb), 0)
    tc = lax.broadcasted_iota(I32, (tb, tb), 1)
    upper = (tr <= tc).astype(BF16)
    incl = jnp.dot(sel.astype(BF16), upper, preferred_element_type=F32)
    pos = incl - 1.0
    pos_ref[...] = jnp.where(sel, pos.astype(I32), -1)
    cnt_ref[0] = jnp.broadcast_to(incl[:, tb - 1:tb], (N_EXPERTS, LANES)).astype(I32)
    p1 = jnp.sum(jnp.where(m1, pos, 0.0), axis=0, keepdims=True)
    p2 = jnp.sum(jnp.where(m2, pos, 0.0), axis=0, keepdims=True)
    ri = lax.broadcasted_iota(I32, (LANES, tb), 0)
    info = jnp.where(ri == 0, p1, jnp.where(ri == 1, p2, jnp.where(
        ri == 2, i1.astype(F32), jnp.where(ri == 3, i2.astype(F32), 0.0))))
    tok_ref[...] = info.T


def _router(x2, g, wrt, br, tb):
    t = x2.shape[0]
    nblk = t // tb
    return pl.pallas_call(
        functools.partial(_router_kernel, tb=tb),
        grid=(nblk,),
        in_specs=[pl.BlockSpec((tb, D_MODEL), lambda i: (i, 0)), _const_spec((1, D_MODEL)),
                  _const_spec((N_EXPERTS, D_MODEL)), _const_spec((N_EXPERTS, 1))],
        out_specs=[pl.BlockSpec((tb, D_MODEL), lambda i: (i, 0)),
                   pl.BlockSpec((N_EXPERTS, tb), lambda i: (0, i)),
                   pl.BlockSpec((N_EXPERTS, tb), lambda i: (0, i)),
                   pl.BlockSpec((1, N_EXPERTS, LANES), lambda i: (i, 0, 0)),
                   pl.BlockSpec((tb, LANES), lambda i: (i, 0))],
        out_shape=[jax.ShapeDtypeStruct((t, D_MODEL), BF16), jax.ShapeDtypeStruct((N_EXPERTS, t), F32),
                   jax.ShapeDtypeStruct((N_EXPERTS, t), I32),
                   jax.ShapeDtypeStruct((nblk, N_EXPERTS, LANES), I32),
                   jax.ShapeDtypeStruct((t, LANES), F32)],
        compiler_params=_params(("arbitrary",)),
        name="router",
    )(x2, g, wrt, br)


SLOT_TILE = 128
COMBINE_ROWS = 256
COMBINE_K = 512


def _first_tile(tb):
    return min(tb, tb * 2 // N_EXPERTS + 32)


def _moe_kernel(cnt_ref, x_ref, hn_ref, pos_ref, comb_ref, tok_ref, wg_ref, wu_ref, wd_ref, out_ref,
                xe, ye, yall, *, tb, nf):
    i, e, f = pl.program_id(0), pl.program_id(1), pl.program_id(2)
    first = _first_tile(tb)

    def rest_tiles(k):
        n_k = cnt_ref[i * N_EXPERTS + k]
        return (jnp.maximum(n_k - first, 0) + SLOT_TILE - 1) // SLOT_TILE

    def seg_rows(k):
        n_k = cnt_ref[i * N_EXPERTS + k]
        return jnp.where(n_k > 0, first + rest_tiles(k) * SLOT_TILE, 0)

    n = cnt_ref[i * N_EXPERTS + e]
    nrest = rest_tiles(e)
    seg_start = [jnp.int32(0)]
    for k in range(N_EXPERTS):
        seg_start.append(seg_start[-1] + seg_rows(k))
    off_e = jnp.int32(0)
    for k in range(1, N_EXPERTS):
        off_e = jnp.where(e == k, seg_start[k], off_e)
    pos_row = pos_ref[pl.ds(e, 1), :]

    def tile_start(j):
        return pl.multiple_of(first + j * SLOT_TILE, 32)

    @pl.when((e == 0) & (f == 0))
    def _clear():
        yall[...] = jnp.zeros(yall.shape, BF16)

    def gather_rows(j0, m):
        onehot = (pos_row == lax.broadcasted_iota(I32, (m, tb), 0) + j0).astype(BF16)
        xe[pl.ds(j0, m), :] = jnp.dot(onehot, hn_ref[...], preferred_element_type=F32).astype(BF16)
        ye[pl.ds(j0, m), :] = jnp.zeros((m, D_MODEL), F32)

    @pl.when((f == 0) & (n > 0))
    def _gather():
        gather_rows(0, first)

        def body(j, carry):
            gather_rows(tile_start(j), SLOT_TILE)
            return carry
        lax.fori_loop(0, nrest, body, 0)

    def expert_rows(j0, m):
        xs = xe[pl.ds(j0, m), :]
        a = jnp.dot(xs, wg_ref[0], preferred_element_type=F32)
        b = jnp.dot(xs, wu_ref[0], preferred_element_type=F32)
        h = (a * jax.nn.sigmoid(a) * b).astype(BF16)
        ye[pl.ds(j0, m), :] += jnp.dot(h, wd_ref[0], preferred_element_type=F32)

    @pl.when(n > 0)
    def _expert():
        expert_rows(0, first)

        def body(j, carry):
            expert_rows(tile_start(j), SLOT_TILE)
            return carry
        lax.fori_loop(0, nrest, body, 0)

    @pl.when((f == nf - 1) & (n > 0))
    def _scale():
        comb_row = comb_ref[pl.ds(e, 1), :]

        def scale_rows(j0, m):
            hit = pos_row == lax.broadcasted_iota(I32, (m, tb), 0) + j0
            gate = jnp.sum(jnp.where(hit, comb_row, 0.0), axis=1, keepdims=True)
            dst = pl.multiple_of(off_e + j0, 32)
            yall[pl.ds(dst, m), :] = (ye[pl.ds(j0, m), :] * gate).astype(BF16)

        scale_rows(0, first)

        def body(j, carry):
            scale_rows(tile_start(j), SLOT_TILE)
            return carry
        lax.fori_loop(0, nrest, body, 0)

    @pl.when((e == N_EXPERTS - 1) & (f == nf - 1))
    def _combine():
        nk = (seg_start[N_EXPERTS] + COMBINE_K - 1) // COMBINE_K
        lane_slot = lax.broadcasted_iota(I32, (COMBINE_ROWS, COMBINE_K), 1).astype(F32)

        def body(r, carry):
            t0 = pl.multiple_of(r * COMBINE_ROWS, COMBINE_ROWS)
            info = tok_ref[pl.ds(t0, COMBINE_ROWS), :]
            s1, s2 = info[:, 0:1], info[:, 1:2]
            i1, i2 = info[:, 2:3], info[:, 3:4]
            for k in range(1, N_EXPERTS):
                start = seg_start[k].astype(F32)
                s1 = s1 + jnp.where(i1 == float(k), start, 0.0)
                s2 = s2 + jnp.where(i2 == float(k), start, 0.0)
            out_ref[pl.ds(t0, COMBINE_ROWS), :] = x_ref[pl.ds(t0, COMBINE_ROWS), :]

            def kstep(kc, carry2):
                k0 = pl.multiple_of(kc * COMBINE_K, COMBINE_K)
                base = k0.astype(F32)
                onehot = ((lane_slot == s1 - base) | (lane_slot == s2 - base)).astype(BF16)
                out_ref[pl.ds(t0, COMBINE_ROWS), :] += jnp.dot(
                    onehot, yall[pl.ds(k0, COMBINE_K), :], preferred_element_type=F32)
                return carry2
            lax.fori_loop(0, nk, kstep, 0)
            return carry
        lax.fori_loop(0, tb // COMBINE_ROWS, body, 0)


def _moe(x2, hn, pos, comb, cnt, tokinfo, wg, wu, wd, tb, nf):
    t = x2.shape[0]
    d_ff = wg.shape[2]
    fc = d_ff // nf
    first = _first_tile(tb)
    xrows = tb + SLOT_TILE
    smax = 2 * tb + N_EXPERTS * first
    smax = (smax + COMBINE_K - 1) // COMBINE_K * COMBINE_K
    single = dict(pipeline_mode=pl.Buffered(1))
    tok = lambda n: pl.BlockSpec((tb, n), lambda i, e, f, c: (i, 0), **single)
    rowinfo = pl.BlockSpec((N_EXPERTS, tb), lambda i, e, f, c: (0, i))
    grid_spec = pltpu.PrefetchScalarGridSpec(
        num_scalar_prefetch=1,
        grid=(t // tb, N_EXPERTS, nf),
        in_specs=[tok(D_MODEL), tok(D_MODEL), rowinfo, rowinfo, tok(LANES),
                  pl.BlockSpec((1, D_MODEL, fc), lambda i, e, f, c: (e, 0, f)),
                  pl.BlockSpec((1, D_MODEL, fc), lambda i, e, f, c: (e, 0, f)),
                  pl.BlockSpec((1, fc, D_MODEL), lambda i, e, f, c: (e, f, 0))],
        out_specs=tok(D_MODEL),
        scratch_shapes=[pltpu.VMEM((xrows, D_MODEL), BF16), pltpu.VMEM((xrows, D_MODEL), F32),
                        pltpu.VMEM((smax, D_MODEL), BF16)],
    )
    return pl.pallas_call(
        functools.partial(_moe_kernel, tb=tb, nf=nf),
        grid_spec=grid_spec,
        out_shape=jax.ShapeDtypeStruct((t, D_MODEL), F32),
        compiler_params=_params(("arbitrary", "arbitrary", "arbitrary")),
        name="moe_experts",
    )(cnt, x2, hn, pos, comb, tokinfo, wg, wu, wd)


def _final_norm_kernel(x_ref, g_ref, out_ref):
    out_ref[...] = _rmsnorm(x_ref[...], g_ref[...])


def _final_norm(x2, g, tm):
    t = x2.shape[0]
    row = pl.BlockSpec((tm, D_MODEL), lambda i: (i, 0))
    return pl.pallas_call(
        _final_norm_kernel,
        grid=(t // tm,),
        in_specs=[row, _const_spec((1, D_MODEL))],
        out_specs=row,
        out_shape=jax.ShapeDtypeStruct((t, D_MODEL), F32),
        compiler_params=_params(("arbitrary",)),
        name="final_norm",
    )(x2, g)


def _tile(n, pref):
    return pref if n % pref == 0 else n


def kernel(x, norm_mix, w_in, conv_qk, b_igate, b_fgate, norm_mlstm, s5_a_re, s5_a_im, s5_log_dt, s5_b_re, s5_b_im, s5_c_re, s5_c_im, s5_d, w_glu, norm_s5, w_out, norm_ffn, ffn_w_gate, ffn_w_up, ffn_w_down, moe_w_router, moe_b_router, moe_w_gate, moe_w_up, moe_w_down, norm_final):
    nb, s, d = x.shape
    assert d == D_MODEL
    t = nb * s
    depth = w_in.shape[0]
    tm = _tile(t, 512)
    lt = _tile(s, 512)
    chunk = _tile(lt, 128)
    tb = _tile(t, 1024)
    nf = 2

    x2 = x.reshape(t, d)
    c0, c1, c2, c3 = D_MLSTM, 2 * D_MLSTM, 3 * D_MLSTM, 4 * D_MLSTM
    cu = c3 + N_GATE_COLS
    for l in range(depth):
        w = w_in[l]
        wg = jnp.zeros((d, LANES), F32).at[:, :N_GATE_COLS].set(w[:, c3:cu]).astype(BF16)
        bg = jnp.zeros((1, LANES), F32).at[0, :HEADS].set(b_igate[l]).at[0, HEADS:N_GATE_COLS].set(b_fgate[l])
        qk, v, o, u, gates = _inproj(
            x2, norm_mix[l][None], w[:, :c1].astype(BF16), w[:, c1:c2].astype(BF16),
            w[:, c2:c3].astype(BF16), w[:, cu:].astype(BF16), wg, bg, tm)
        r3 = lambda a: a.reshape(nb, s, a.shape[-1])
        hm = _mlstm(r3(qk), r3(v), r3(o), r3(gates), conv_qk[l], norm_mlstm[l][None], lt, chunk)
        wb, a_re_t, a_im_t, wc = _s5_discretise(s5_a_re[l], s5_a_im[l], s5_log_dt[l], s5_b_re[l], s5_b_im[l],
                                                s5_c_re[l], s5_c_im[l])
        ys = _s5(r3(u), wb, a_re_t, a_im_t, wc, s5_d[l][None], w_glu[l].astype(BF16), norm_s5[l][None], lt)
        x2 = _outproj(x2, hm.reshape(t, D_MLSTM), ys.reshape(t, D_S5),
                      w_out[l, :D_MLSTM].astype(BF16), w_out[l, D_MLSTM:].astype(BF16), tm)
        j = l // 2
        if l % 2 == 0:
            x2 = _ffn(x2, norm_ffn[l][None], ffn_w_gate[j].astype(BF16), ffn_w_up[j].astype(BF16),
                      ffn_w_down[j].astype(BF16), tm)
        else:
            hn, comb, pos, cnt, tokinfo = _router(x2, norm_ffn[l][None], moe_w_router[j].T,
                                                  moe_b_router[j][:, None], tb)
            x2 = _moe(x2, hn, pos, comb, cnt[:, :, 0].reshape(-1), tokinfo, moe_w_gate[j].astype(BF16),
                      moe_w_up[j].astype(BF16), moe_w_down[j].astype(BF16), tb, nf)
    out = _final_norm(x2, norm_final[None], tm)
    return out.reshape(nb, s, d)
```

```python
import functools

import jax
import jax.numpy as jnp
from jax import lax
from jax.experimental import pallas as pl
from jax.experimental.pallas import tpu as pltpu

F32, BF16, I32 = jnp.float32, jnp.bfloat16, jnp.int32
HIGHEST = lax.Precision.HIGHEST

RMS_EPS = 1e-6
D_MODEL = 1024
D_MLSTM = 512
D_S5 = 512
HEADS = 4
HEAD_DIM = 128
CONV_WIDTH = 4
S5_GROUP = 16
S5_GROUPS = 32
S5_STATE = 64
N_EXPERTS = 8
N_GATE_COLS = 2 * HEADS

LANES = 128
SUBLANES = 8
VMEM_LIMIT = 56 * 1024 * 1024

S5_FOLD = SUBLANES
S5_GROUPS_PER_ROW = S5_GROUPS // S5_FOLD
S5_HALF = S5_GROUPS_PER_ROW * S5_STATE
S5_CH_PER_ROW = S5_GROUPS_PER_ROW * S5_GROUP
S5_SLABS = D_S5 // LANES
S5_C_ROWS = 4


def _params(sem):
    return pltpu.CompilerParams(dimension_semantics=sem, vmem_limit_bytes=VMEM_LIMIT)


def _const_spec(shape):
    nd = len(shape)
    return pl.BlockSpec(shape, lambda *_: (0,) * nd, pipeline_mode=pl.Buffered(1))


def _rmsnorm(x, g):
    y = x * lax.rsqrt(jnp.mean(x * x, axis=-1, keepdims=True) + RMS_EPS)
    return y * g


def _log_sigmoid(x):
    return jnp.minimum(x, 0.0) - jnp.log1p(jnp.exp(-jnp.abs(x)))


def _inproj_kernel(x_ref, g_ref, wqk_ref, wv_ref, wo_ref, wu_ref, wg_ref, bg_ref,
                   qk_ref, v_ref, o_ref, u_ref, gate_ref):
    hn = _rmsnorm(x_ref[...], g_ref[...]).astype(BF16)
    qk_ref[...] = jnp.dot(hn, wqk_ref[...], preferred_element_type=F32).astype(BF16)
    v_ref[...] = jnp.dot(hn, wv_ref[...], preferred_element_type=F32).astype(BF16)
    o_ref[...] = jnp.dot(hn, wo_ref[...], preferred_element_type=F32).astype(BF16)
    u_ref[...] = jnp.dot(hn, wu_ref[...], preferred_element_type=F32).astype(BF16)
    gp = jnp.dot(hn, wg_ref[...], preferred_element_type=F32) + bg_ref[...]
    lane = lax.broadcasted_iota(I32, gp.shape, 1)
    gate_ref[...] = jnp.where(lane < HEADS, gp, _log_sigmoid(gp))


def _inproj(x2, g, wqk, wv, wo, wu, wg, bg, tm):
    t = x2.shape[0]
    row = lambda n: pl.BlockSpec((tm, n), lambda i: (i, 0))
    return pl.pallas_call(
        _inproj_kernel,
        grid=(t // tm,),
        in_specs=[row(D_MODEL), _const_spec((1, D_MODEL)), _const_spec(wqk.shape), _const_spec(wv.shape),
                  _const_spec(wo.shape), _const_spec(wu.shape), _const_spec(wg.shape), _const_spec((1, LANES))],
        out_specs=[row(2 * D_MLSTM), row(D_MLSTM), row(D_MLSTM), row(D_S5), row(LANES)],
        out_shape=[jax.ShapeDtypeStruct((t, 2 * D_MLSTM), BF16), jax.ShapeDtypeStruct((t, D_MLSTM), BF16),
                   jax.ShapeDtypeStruct((t, D_MLSTM), BF16), jax.ShapeDtypeStruct((t, D_S5), BF16),
                   jax.ShapeDtypeStruct((t, LANES), F32)],
        compiler_params=_params(("arbitrary",)),
        name="inproj",
    )(x2, g, wqk, wv, wo, wu, wg, bg)


CONV_ROWS = 32
HALO = SUBLANES


def _mlstm_kernel(qk_ref, v_ref, o_ref, g_ref, cw_ref, nw_ref, out_ref,
                  xbuf, qkact, cstate, mstate, *, nb, lt, chunk):
    i = pl.program_id(0)

    @pl.when(i == 0)
    def _init():
        xbuf[:, 0:HALO, :] = jnp.zeros((nb, HALO, 2 * D_MLSTM), F32)
        cstate[...] = jnp.zeros(cstate.shape, F32)
        mstate[...] = jnp.zeros(mstate.shape, F32)

    lane = lax.broadcasted_iota(I32, (1, 2 * D_MLSTM), 1)
    qk_scale = jnp.where(lane < D_MLSTM, 1.0, HEAD_DIM ** -0.5).astype(F32)
    for b in range(nb):
        xbuf[b, HALO:HALO + lt, :] = qk_ref[b].astype(F32)

        def conv_block(r, carry, b=b):
            r0 = pl.multiple_of(r * CONV_ROWS, CONV_ROWS)
            win = xbuf[b, pl.ds(r0, CONV_ROWS + HALO), :]
            acc = win[HALO:, :] * cw_ref[CONV_WIDTH - 1:CONV_WIDTH, :]
            for j in range(1, CONV_WIDTH):
                acc = acc + win[HALO - j:HALO - j + CONV_ROWS, :] * cw_ref[CONV_WIDTH - 1 - j:CONV_WIDTH - j, :]
            act = acc * jax.nn.sigmoid(acc) * qk_scale
            qkact[b, pl.ds(r0, CONV_ROWS), :] = act.astype(BF16)
            return carry

        lax.fori_loop(0, lt // CONV_ROWS, conv_block, 0)
        xbuf[b, 0:HALO, :] = xbuf[b, lt:lt + HALO, :]

    L = chunk
    rr = lax.broadcasted_iota(I32, (L, L), 0)
    cc = lax.broadcasted_iota(I32, (L, L), 1)
    causal = rr >= cc
    tri = causal.astype(F32)
    ones_col = (lax.broadcasted_iota(I32, (L, HEAD_DIM), 1) == 0).astype(BF16)

    def chunk_body(c, carry):
        r0 = pl.multiple_of(c * L, L)
        for b in range(nb):
            gates = g_ref[b, pl.ds(r0, L), :]
            csum = jnp.dot(tri, gates, preferred_element_type=F32, precision=HIGHEST)
            gates_t = gates.T
            csum_t = csum.T
            for h in range(HEADS):
                sl = slice(h * HEAD_DIM, (h + 1) * HEAD_DIM)
                slk = slice(D_MLSTM + h * HEAD_DIM, D_MLSTM + (h + 1) * HEAD_DIM)
                li_col = gates[:, h:h + 1]
                li_row = gates_t[h:h + 1, :]
                b_col = csum[:, HEADS + h:HEADS + h + 1]
                b_row = csum_t[HEADS + h:HEADS + h + 1, :]
                m_prev = mstate[b * HEADS + h, 0:1, 0:1]

                log_d = jnp.where(causal, b_col - b_row + li_row, -jnp.inf)
                m_inter = b_col + m_prev
                m_t = jnp.maximum(m_inter, jnp.max(log_d, axis=-1, keepdims=True))
                d = jnp.exp(log_d - m_t)
                inter = jnp.exp(m_inter - m_t)

                q = qkact[b, pl.ds(r0, L), sl]
                k = qkact[b, pl.ds(r0, L), slk]
                v = v_ref[b, pl.ds(r0, L), sl]
                vext = jnp.concatenate([v, ones_col], axis=1)
                s = lax.dot_general(q, k, (((1,), (1,)), ((), ())), preferred_element_type=F32)
                p = (s * d).astype(BF16)
                cext = cstate[b * HEADS + h]
                r = (jnp.dot(p, vext, preferred_element_type=F32)
                     + inter * jnp.dot(q, cext.astype(BF16), preferred_element_type=F32))
                num = r[:, :HEAD_DIM]
                den = r[:, HEAD_DIM:HEAD_DIM + 1]
                hh = num / jnp.maximum(jnp.abs(den), jnp.exp(-m_t))
                hh = hh * lax.rsqrt(jnp.mean(hh * hh, axis=-1, keepdims=True) + RMS_EPS)
                hh = hh * nw_ref[:, sl]
                og = jax.nn.sigmoid(o_ref[b, pl.ds(r0, L), sl].astype(F32))
                out_ref[b, pl.ds(r0, L), sl] = (og * hh).astype(BF16)

                b_end = b_col[L - 1:L, :]
                log_w = b_end - b_col + li_col
                m_new = jnp.maximum(b_end + m_prev, jnp.max(log_w, axis=0, keepdims=True))
                w = jnp.exp(log_w - m_new)
                decay = jnp.exp(b_end + m_prev - m_new)
                kw = (k.astype(F32) * w).astype(BF16)
                upd = lax.dot_general(kw, vext, (((0,), (0,)), ((), ())), preferred_element_type=F32)
                cstate[b * HEADS + h] = decay * cext + upd
                mstate[b * HEADS + h] = jnp.broadcast_to(m_new, (SUBLANES, LANES))
        return carry

    lax.fori_loop(0, lt // L, chunk_body, 0)


def _mlstm(qk, v, o, gates, conv_w, norm_w, lt, chunk):
    nb, s, _ = qk.shape
    blk = lambda n: pl.BlockSpec((nb, lt, n), lambda i: (0, i, 0))
    kern = functools.partial(_mlstm_kernel, nb=nb, lt=lt, chunk=chunk)
    return pl.pallas_call(
        kern,
        grid=(s // lt,),
        in_specs=[blk(2 * D_MLSTM), blk(D_MLSTM), blk(D_MLSTM), blk(LANES),
                  _const_spec((CONV_WIDTH, 2 * D_MLSTM)), _const_spec((1, D_MLSTM))],
        out_specs=blk(D_MLSTM),
        out_shape=jax.ShapeDtypeStruct((nb, s, D_MLSTM), BF16),
        scratch_shapes=[pltpu.VMEM((nb, lt + HALO, 2 * D_MLSTM), F32),
                        pltpu.VMEM((nb, lt, 2 * D_MLSTM), BF16),
                        pltpu.VMEM((nb * HEADS, HEAD_DIM, 2 * HEAD_DIM), F32),
                        pltpu.VMEM((nb * HEADS, SUBLANES, LANES), F32)],
        compiler_params=_params(("arbitrary",)),
        name="mlstm",
    )(qk, v, o, gates, conv_w, norm_w)


S5_SUB = 128
S5_UNROLL = 8


def _gelu_tanh(x):
    return 0.5 * x * (1.0 + jnp.tanh(0.7978845608028654 * (x + 0.044715 * (x * x * x))))


def _s5_kernel(u_ref, wb_ref, are_ref, aim_ref, wc_ref, dsk_ref, wglu_ref, nw_ref, out_ref,
               bu, st, *, nb, lt):
    i = pl.program_id(0)

    @pl.when(i == 0)
    def _init():
        st[...] = jnp.zeros(st.shape, F32)

    rows = S5_SUB * S5_FOLD
    nsub = lt // S5_SUB
    pairs = S5_FOLD // S5_SLABS

    for b in range(nb):
        def bproj(s, carry, b=b):
            t0 = pl.multiple_of(s * S5_SUB, S5_SUB)
            base = pl.multiple_of(s * rows, rows)
            for q in range(S5_SLABS):
                uq = u_ref[b, pl.ds(t0, S5_SUB), q * LANES:(q + 1) * LANES]
                for r in range(pairs * q, pairs * (q + 1)):
                    res = jnp.dot(uq, wb_ref[r], preferred_element_type=F32)
                    for k in range(S5_SLABS):
                        bu[b, k, pl.ds(base + r, S5_SUB, stride=S5_FOLD), :] = res[:, k * LANES:(k + 1) * LANES]
            return carry
        lax.fori_loop(0, nsub, bproj, 0)

    nh = S5_SLABS // 2
    a_re = [are_ref[:, j * LANES:(j + 1) * LANES] for j in range(nh)]
    a_im = [aim_ref[:, j * LANES:(j + 1) * LANES] for j in range(nh)]

    def step(t, carry):
        r0 = pl.multiple_of(t * S5_FOLD, S5_FOLD)
        new = []
        for b in range(nb):
            nxt = [None] * S5_SLABS
            for j in range(nh):
                x_re, x_im = carry[b][j], carry[b][nh + j]
                n_re = a_re[j] * x_re - a_im[j] * x_im + bu[b, j, pl.ds(r0, S5_FOLD), :]
                n_im = a_re[j] * x_im + a_im[j] * x_re + bu[b, nh + j, pl.ds(r0, S5_FOLD), :]
                bu[b, j, pl.ds(r0, S5_FOLD), :] = n_re
                bu[b, nh + j, pl.ds(r0, S5_FOLD), :] = n_im
                nxt[j], nxt[nh + j] = n_re, n_im
            new.append(tuple(nxt))
        return tuple(new)

    init = tuple(tuple(st[b, k] for k in range(S5_SLABS)) for b in range(nb))
    fin = lax.fori_loop(0, lt, step, init, unroll=S5_UNROLL)
    for b in range(nb):
        for k in range(S5_SLABS):
            st[b, k] = fin[b][k]

    for b in range(nb):
        def cproj(s, carry, b=b):
            t0 = pl.multiple_of(s * S5_SUB, S5_SUB)
            base = pl.multiple_of(s * rows, rows)
            cols = []
            for q in range(S5_FOLD // S5_C_ROWS):
                acc = None
                for r in range(S5_C_ROWS * q, S5_C_ROWS * (q + 1)):
                    xr = jnp.concatenate(
                        [bu[b, k, pl.ds(base + r, S5_SUB, stride=S5_FOLD), :] for k in range(S5_SLABS)],
                        axis=1).astype(BF16)
                    part = jnp.dot(xr, wc_ref[r], preferred_element_type=F32)
                    acc = part if acc is None else acc + part
                cols.append(acc)
            y = jnp.concatenate(cols, axis=1)
            y = y + dsk_ref[...] * u_ref[b, pl.ds(t0, S5_SUB), :].astype(F32)
            g = _gelu_tanh(y)
            z = jnp.dot(g.astype(BF16), wglu_ref[...], preferred_element_type=F32)
            ys = g * jax.nn.sigmoid(z)
            ys = ys * lax.rsqrt(jnp.mean(ys * ys, axis=-1, keepdims=True) + RMS_EPS) * nw_ref[...]
            out_ref[b, pl.ds(t0, S5_SUB), :] = ys.astype(BF16)
            return carry
        lax.fori_loop(0, nsub, cproj, 0)


def _s5(u, wb, a_re, a_im, wc, dskip, wglu, norm_w, lt):
    nb, s, _ = u.shape
    blk = pl.BlockSpec((nb, lt, D_S5), lambda i: (0, i, 0))
    kern = functools.partial(_s5_kernel, nb=nb, lt=lt)
    return pl.pallas_call(
        kern,
        grid=(s // lt,),
        in_specs=[blk, _const_spec(wb.shape), _const_spec((S5_FOLD, S5_HALF)), _const_spec((S5_FOLD, S5_HALF)),
                  _const_spec(wc.shape), _const_spec((1, D_S5)), _const_spec((D_S5, D_S5)),
                  _const_spec((1, D_S5))],
        out_specs=blk,
        out_shape=jax.ShapeDtypeStruct((nb, s, D_S5), BF16),
        scratch_shapes=[pltpu.VMEM((nb, S5_SLABS, lt * S5_FOLD, LANES), F32),
                        pltpu.VMEM((nb, S5_SLABS, S5_FOLD, LANES), F32)],
        compiler_params=_params(("arbitrary",)),
        name="s5",
    )(u, wb, a_re, a_im, wc, dskip, wglu, norm_w)


def _s5_discretise(a_re, a_im, log_dt, b_re, b_im, c_re, c_im):
    dt = jnp.exp(log_dt)[:, None]
    mag = jnp.exp(a_re * dt)
    ab_re = mag * jnp.cos(a_im * dt)
    ab_im = mag * jnp.sin(a_im * dt)
    lam2 = a_re * a_re + a_im * a_im
    z_re = ab_re - 1.0
    f_re = (z_re * a_re + ab_im * a_im) / lam2
    f_im = (ab_im * a_re - z_re * a_im) / lam2
    bb_re = f_re[..., None] * b_re - f_im[..., None] * b_im
    bb_im = f_re[..., None] * b_im + f_im[..., None] * b_re

    gl = S5_GROUPS_PER_ROW
    a_re_t = ab_re.reshape(S5_FOLD, S5_HALF)
    a_im_t = ab_im.reshape(S5_FOLD, S5_HALF)

    def pack_b(bb):
        x = bb.reshape(S5_FOLD, gl, S5_STATE, S5_GROUP).transpose(0, 1, 3, 2)
        eye = jnp.eye(gl, dtype=F32)
        return jnp.einsum('rghp,gk->rghkp', x, eye).reshape(D_S5, S5_HALF)
    wb = jnp.concatenate([pack_b(bb_re), pack_b(bb_im)], axis=1)

    def pack_c(c):
        x = c.reshape(S5_FOLD, gl, S5_GROUP, S5_STATE)
        eye = jnp.eye(gl, dtype=F32)
        return jnp.einsum('rghp,gk->kprgh', x, eye).reshape(S5_HALF, D_S5)
    wc = jnp.concatenate([pack_c(c_re), -pack_c(c_im)], axis=0)

    pairs = S5_FOLD // S5_SLABS
    place_b = jnp.eye(pairs, dtype=F32)[jnp.arange(S5_FOLD) % pairs]
    wb_r = wb.reshape(S5_FOLD, S5_CH_PER_ROW, D_S5)
    wb_r = jnp.einsum('rcl,rp->rpcl', wb_r, place_b).reshape(S5_FOLD, LANES, D_S5)
    place_c = jnp.eye(S5_C_ROWS, dtype=F32)[jnp.arange(S5_FOLD) % S5_C_ROWS]
    wc_r = wc.reshape(D_S5, S5_FOLD, S5_CH_PER_ROW).transpose(1, 0, 2)
    wc_r = jnp.einsum('rlc,rp->rlpc', wc_r, place_c).reshape(S5_FOLD, D_S5, S5_C_ROWS * S5_CH_PER_ROW)
    return wb_r.astype(BF16), a_re_t, a_im_t, wc_r.astype(BF16)


def _outproj_kernel(x_ref, hm_ref, ys_ref, w1_ref, w2_ref, out_ref):
    out_ref[...] = (x_ref[...]
                    + jnp.dot(hm_ref[...], w1_ref[...], preferred_element_type=F32)
                    + jnp.dot(ys_ref[...], w2_ref[...], preferred_element_type=F32))


def _outproj(x2, hm, ys, w1, w2, tm):
    t = x2.shape[0]
    row = lambda n: pl.BlockSpec((tm, n), lambda i: (i, 0))
    return pl.pallas_call(
        _outproj_kernel,
        grid=(t // tm,),
        in_specs=[row(D_MODEL), row(D_MLSTM), row(D_S5), _const_spec(w1.shape), _const_spec(w2.shape)],
        out_specs=row(D_MODEL),
        out_shape=jax.ShapeDtypeStruct((t, D_MODEL), F32),
        compiler_params=_params(("arbitrary",)),
        name="outproj",
    )(x2, hm, ys, w1, w2)


FF_CHUNK = 256


def _ffn_kernel(x_ref, g_ref, wg_ref, wu_ref, wd_ref, out_ref, hn_s, *, d_ff):
    x = x_ref[...]
    hn_s[...] = _rmsnorm(x, g_ref[...]).astype(BF16)
    out_ref[...] = x
    for c in range(d_ff // FF_CHUNK):
        cs = slice(c * FF_CHUNK, (c + 1) * FF_CHUNK)
        hn = hn_s[...]
        a = jnp.dot(hn, wg_ref[:, cs], preferred_element_type=F32)
        b = jnp.dot(hn, wu_ref[:, cs], preferred_element_type=F32)
        h = (a * jax.nn.sigmoid(a) * b).astype(BF16)
        out_ref[...] += jnp.dot(h, wd_ref[cs, :], preferred_element_type=F32)


def _ffn(x2, g, wg, wu, wd, tm):
    t = x2.shape[0]
    d_ff = wg.shape[1]
    row = pl.BlockSpec((tm, D_MODEL), lambda i: (i, 0))
    return pl.pallas_call(
        functools.partial(_ffn_kernel, d_ff=d_ff),
        grid=(t // tm,),
        in_specs=[row, _const_spec((1, D_MODEL)), _const_spec(wg.shape), _const_spec(wu.shape),
                  _const_spec(wd.shape)],
        out_specs=row,
        out_shape=jax.ShapeDtypeStruct((t, D_MODEL), F32),
        scratch_shapes=[pltpu.VMEM((tm, D_MODEL), BF16)],
        compiler_params=_params(("arbitrary",)),
        name="ffn_dense",
    )(x2, g, wg, wu, wd)


def _router_kernel(x_ref, g_ref, wrt_ref, br_ref, hn_ref, comb_ref, pos_ref, cnt_ref, tok_ref, *, tb):
    hn = _rmsnorm(x_ref[...], g_ref[...])
    hn_ref[...] = hn.astype(BF16)
    logits = lax.dot_general(wrt_ref[...], hn, (((1,), (1,)), ((), ())),
                             preferred_element_type=F32, precision=HIGHEST) + br_ref[...]
    eidx = lax.broadcasted_iota(I32, logits.shape, 0)
    v1 = jnp.max(logits, axis=0, keepdims=True)
    i1 = jnp.min(jnp.where(logits == v1, eidx, N_EXPERTS), axis=0, keepdims=True)
    m1 = eidx == i1
    rest = jnp.where(m1, -jnp.inf, logits)
    v2 = jnp.max(rest, axis=0, keepdims=True)
    i2 = jnp.min(jnp.where(rest == v2, eidx, N_EXPERTS), axis=0, keepdims=True)
    m2 = eidx == i2
    e2 = jnp.exp(v2 - v1)
    g1 = 1.0 / (1.0 + e2)
    g2 = e2 / (1.0 + e2)
    comb_ref[...] = jnp.where(m1, g1, 0.0) + jnp.where(m2, g2, 0.0)
    sel = m1 | m2
    tr = lax.broadcasted_iota(I32, (tb, tb), 0)
    tc = lax.broadcasted_iota(I32, (tb, tb), 1)
    upper = (tr <= tc).astype(BF16)
    incl = jnp.dot(sel.astype(BF16), upper, preferred_element_type=F32)
    pos = incl - 1.0
    pos_ref[...] = jnp.where(sel, pos.astype(I32), -1)
    cnt_ref[0] = jnp.broadcast_to(incl[:, tb - 1:tb], (N_EXPERTS, LANES)).astype(I32)
    p1 = jnp.sum(jnp.where(m1, pos, 0.0), axis=0, keepdims=True)
    p2 = jnp.sum(jnp.where(m2, pos, 0.0), axis=0, keepdims=True)
    ri = lax.broadcasted_iota(I32, (LANES, tb), 0)
    info = jnp.where(ri == 0, p1, jnp.where(ri == 1, p2, jnp.where(
        ri == 2, i1.astype(F32), jnp.where(ri == 3, i2.astype(F32), 0.0))))
    tok_ref[...] = info.T


def _router(x2, g, wrt, br, tb):
    t = x2.shape[0]
    nblk = t // tb
    return pl.pallas_call(
        functools.partial(_router_kernel, tb=tb),
        grid=(nblk,),
        in_specs=[pl.BlockSpec((tb, D_MODEL), lambda i: (i, 0)), _const_spec((1, D_MODEL)),
                  _const_spec((N_EXPERTS, D_MODEL)), _const_spec((N_EXPERTS, 1))],
        out_specs=[pl.BlockSpec((tb, D_MODEL), lambda i: (i, 0)),
                   pl.BlockSpec((N_EXPERTS, tb), lambda i: (0, i)),
                   pl.BlockSpec((N_EXPERTS, tb), lambda i: (0, i)),
                   pl.BlockSpec((1, N_EXPERTS, LANES), lambda i: (i, 0, 0)),
                   pl.BlockSpec((tb, LANES), lambda i: (i, 0))],
        out_shape=[jax.ShapeDtypeStruct((t, D_MODEL), BF16), jax.ShapeDtypeStruct((N_EXPERTS, t), F32),
                   jax.ShapeDtypeStruct((N_EXPERTS, t), I32),
                   jax.ShapeDtypeStruct((nblk, N_EXPERTS, LANES), I32),
                   jax.ShapeDtypeStruct((t, LANES), F32)],
        compiler_params=_params(("arbitrary",)),
        name="router",
    )(x2, g, wrt, br)


SLOT_TILE = 128
COMBINE_ROWS = 256
COMBINE_K = 512


def _first_tile(tb):
    return min(tb, tb * 2 // N_EXPERTS + 32)


def _moe_kernel(cnt_ref, x_ref, hn_ref, pos_ref, comb_ref, tok_ref, wg_ref, wu_ref, wd_ref, out_ref,
                xe, ye, yall, *, tb, nf):
    i, e, f = pl.program_id(0), pl.program_id(1), pl.program_id(2)
    first = _first_tile(tb)

    def rest_tiles(k):
        n_k = cnt_ref[i * N_EXPERTS + k]
        return (jnp.maximum(n_k - first, 0) + SLOT_TILE - 1) // SLOT_TILE

    def seg_rows(k):
        n_k = cnt_ref[i * N_EXPERTS + k]
        return jnp.where(n_k > 0, first + rest_tiles(k) * SLOT_TILE, 0)

    n = cnt_ref[i * N_EXPERTS + e]
    nrest = rest_tiles(e)
    seg_start = [jnp.int32(0)]
    for k in range(N_EXPERTS):
        seg_start.append(seg_start[-1] + seg_rows(k))
    off_e = jnp.int32(0)
    for k in range(1, N_EXPERTS):
        off_e = jnp.where(e == k, seg_start[k], off_e)
    pos_row = pos_ref[pl.ds(e, 1), :]

    def tile_start(j):
        return pl.multiple_of(first + j * SLOT_TILE, 32)

    @pl.when((e == 0) & (f == 0))
    def _clear():
        yall[...] = jnp.zeros(yall.shape, BF16)

    def gather_rows(j0, m):
        onehot = (pos_row == lax.broadcasted_iota(I32, (m, tb), 0) + j0).astype(BF16)
        xe[pl.ds(j0, m), :] = jnp.dot(onehot, hn_ref[...], preferred_element_type=F32).astype(BF16)
        ye[pl.ds(j0, m), :] = jnp.zeros((m, D_MODEL), F32)

    @pl.when((f == 0) & (n > 0))
    def _gather():
        gather_rows(0, first)

        def body(j, carry):
            gather_rows(tile_start(j), SLOT_TILE)
            return carry
        lax.fori_loop(0, nrest, body, 0)

    def expert_rows(j0, m):
        xs = xe[pl.ds(j0, m), :]
        a = jnp.dot(xs, wg_ref[0, 0], preferred_element_type=F32)
        b = jnp.dot(xs, wu_ref[0, 0], preferred_element_type=F32)
        h = (a * jax.nn.sigmoid(a) * b).astype(BF16)
        ye[pl.ds(j0, m), :] += jnp.dot(h, wd_ref[0, 0], preferred_element_type=F32)

    @pl.when(n > 0)
    def _expert():
        expert_rows(0, first)

        def body(j, carry):
            expert_rows(tile_start(j), SLOT_TILE)
            return carry
        lax.fori_loop(0, nrest, body, 0)

    @pl.when((f == nf - 1) & (n > 0))
    def _scale():
        comb_row = comb_ref[pl.ds(e, 1), :]

        def scale_rows(j0, m):
            hit = pos_row == lax.broadcasted_iota(I32, (m, tb), 0) + j0
            gate = jnp.sum(jnp.where(hit, comb_row, 0.0), axis=1, keepdims=True)
            dst = pl.multiple_of(off_e + j0, 32)
            yall[pl.ds(dst, m), :] = (ye[pl.ds(j0, m), :] * gate).astype(BF16)

        scale_rows(0, first)

        def body(j, carry):
            scale_rows(tile_start(j), SLOT_TILE)
            return carry
        lax.fori_loop(0, nrest, body, 0)

    @pl.when((e == N_EXPERTS - 1) & (f == nf - 1))
    def _combine():
        nk = (seg_start[N_EXPERTS] + COMBINE_K - 1) // COMBINE_K
        lane_slot = lax.broadcasted_iota(I32, (COMBINE_ROWS, COMBINE_K), 1).astype(F32)

        def body(r, carry):
            t0 = pl.multiple_of(r * COMBINE_ROWS, COMBINE_ROWS)
            info = tok_ref[pl.ds(t0, COMBINE_ROWS), :]
            s1, s2 = info[:, 0:1], info[:, 1:2]
            i1, i2 = info[:, 2:3], info[:, 3:4]
            for k in range(1, N_EXPERTS):
                start = seg_start[k].astype(F32)
                s1 = s1 + jnp.where(i1 == float(k), start, 0.0)
                s2 = s2 + jnp.where(i2 == float(k), start, 0.0)
            out_ref[pl.ds(t0, COMBINE_ROWS), :] = x_ref[pl.ds(t0, COMBINE_ROWS), :]

            def kstep(kc, carry2):
                k0 = pl.multiple_of(kc * COMBINE_K, COMBINE_K)
                base = k0.astype(F32)
                onehot = ((lane_slot == s1 - base) | (lane_slot == s2 - base)).astype(BF16)
                out_ref[pl.ds(t0, COMBINE_ROWS), :] += jnp.dot(
                    onehot, yall[pl.ds(k0, COMBINE_K), :], preferred_element_type=F32)
                return carry2
            lax.fori_loop(0, nk, kstep, 0)
            return carry
        lax.fori_loop(0, tb // COMBINE_ROWS, body, 0)


def _moe(x2, hn, pos, comb, cnt, tokinfo, wg, wu, wd, tb, nf):
    t = x2.shape[0]
    fc = wg.shape[3]
    first = _first_tile(tb)
    xrows = tb + SLOT_TILE
    smax = 2 * tb + N_EXPERTS * first
    smax = (smax + COMBINE_K - 1) // COMBINE_K * COMBINE_K
    single = dict(pipeline_mode=pl.Buffered(1))
    tok = lambda n: pl.BlockSpec((tb, n), lambda i, e, f, c: (i, 0), **single)
    rowinfo = pl.BlockSpec((N_EXPERTS, tb), lambda i, e, f, c: (0, i))
    grid_spec = pltpu.PrefetchScalarGridSpec(
        num_scalar_prefetch=1,
        grid=(t // tb, N_EXPERTS, nf),
        in_specs=[tok(D_MODEL), tok(D_MODEL), rowinfo, rowinfo, tok(LANES),
                  pl.BlockSpec((1, 1, D_MODEL, fc), lambda i, e, f, c: (e, f, 0, 0)),
                  pl.BlockSpec((1, 1, D_MODEL, fc), lambda i, e, f, c: (e, f, 0, 0)),
                  pl.BlockSpec((1, 1, fc, D_MODEL), lambda i, e, f, c: (e, f, 0, 0))],
        out_specs=tok(D_MODEL),
        scratch_shapes=[pltpu.VMEM((xrows, D_MODEL), BF16), pltpu.VMEM((xrows, D_MODEL), F32),
                        pltpu.VMEM((smax, D_MODEL), BF16)],
    )
    return pl.pallas_call(
        functools.partial(_moe_kernel, tb=tb, nf=nf),
        grid_spec=grid_spec,
        out_shape=jax.ShapeDtypeStruct((t, D_MODEL), F32),
        compiler_params=_params(("arbitrary", "arbitrary", "arbitrary")),
        name="moe_experts",
    )(cnt, x2, hn, pos, comb, tokinfo, wg, wu, wd)


def _final_norm_kernel(x_ref, g_ref, out_ref):
    out_ref[...] = _rmsnorm(x_ref[...], g_ref[...])


def _final_norm(x2, g, tm):
    t = x2.shape[0]
    row = pl.BlockSpec((tm, D_MODEL), lambda i: (i, 0))
    return pl.pallas_call(
        _final_norm_kernel,
        grid=(t // tm,),
        in_specs=[row, _const_spec((1, D_MODEL))],
        out_specs=row,
        out_shape=jax.ShapeDtypeStruct((t, D_MODEL), F32),
        compiler_params=_params(("arbitrary",)),
        name="final_norm",
    )(x2, g)


def _tile(n, pref):
    return pref if n % pref == 0 else n


def kernel(x, norm_mix, w_in, conv_qk, b_igate, b_fgate, norm_mlstm, s5_a_re, s5_a_im, s5_log_dt, s5_b_re, s5_b_im, s5_c_re, s5_c_im, s5_d, w_glu, norm_s5, w_out, norm_ffn, ffn_w_gate, ffn_w_up, ffn_w_down, moe_w_router, moe_b_router, moe_w_gate, moe_w_up, moe_w_down, norm_final):
    nb, s, d = x.shape
    assert d == D_MODEL
    t = nb * s
    depth = w_in.shape[0]
    tm = _tile(t, 512)
    lt = _tile(s, 512)
    chunk = _tile(lt, 128)
    tb = _tile(t, 1024)
    nf = 2

    x2 = x.reshape(t, d)
    c0, c1, c2, c3 = D_MLSTM, 2 * D_MLSTM, 3 * D_MLSTM, 4 * D_MLSTM
    cu = c3 + N_GATE_COLS
    for l in range(depth):
        w = w_in[l]
        wg = jnp.zeros((d, LANES), F32).at[:, :N_GATE_COLS].set(w[:, c3:cu]).astype(BF16)
        bg = jnp.zeros((1, LANES), F32).at[0, :HEADS].set(b_igate[l]).at[0, HEADS:N_GATE_COLS].set(b_fgate[l])
        qk, v, o, u, gates = _inproj(
            x2, norm_mix[l][None], w[:, :c1].astype(BF16), w[:, c1:c2].astype(BF16),
            w[:, c2:c3].astype(BF16), w[:, cu:].astype(BF16), wg, bg, tm)
        r3 = lambda a: a.reshape(nb, s, a.shape[-1])
        hm = _mlstm(r3(qk), r3(v), r3(o), r3(gates), conv_qk[l], norm_mlstm[l][None], lt, chunk)
        wb, a_re_t, a_im_t, wc = _s5_discretise(s5_a_re[l], s5_a_im[l], s5_log_dt[l], s5_b_re[l], s5_b_im[l],
                                                s5_c_re[l], s5_c_im[l])
        ys = _s5(r3(u), wb, a_re_t, a_im_t, wc, s5_d[l][None], w_glu[l].astype(BF16), norm_s5[l][None], lt)
        x2 = _outproj(x2, hm.reshape(t, D_MLSTM), ys.reshape(t, D_S5),
                      w_out[l, :D_MLSTM].astype(BF16), w_out[l, D_MLSTM:].astype(BF16), tm)
        j = l // 2
        if l % 2 == 0:
            x2 = _ffn(x2, norm_ffn[l][None], ffn_w_gate[j].astype(BF16), ffn_w_up[j].astype(BF16),
                      ffn_w_down[j].astype(BF16), tm)
        else:
            hn, comb, pos, cnt, tokinfo = _router(x2, norm_ffn[l][None], moe_w_router[j].T,
                                                  moe_b_router[j][:, None], tb)
            d_ff = moe_w_gate.shape[-1]
            chunked = lambda w: w.astype(BF16).reshape(N_EXPERTS, d, nf, d_ff // nf).transpose(0, 2, 1, 3)
            x2 = _moe(x2, hn, pos, comb, cnt[:, :, 0].reshape(-1), tokinfo, chunked(moe_w_gate[j]),
                      chunked(moe_w_up[j]), moe_w_down[j].astype(BF16).reshape(N_EXPERTS, nf, d_ff // nf, d),
                      tb, nf)
    out = _final_norm(x2, norm_final[None], tm)
    return out.reshape(nb, s, d)
```

```python
import functools

import jax
import jax.numpy as jnp
from jax import lax
from jax.experimental import pallas as pl
from jax.experimental.pallas import tpu as pltpu

F32, BF16, I32 = jnp.float32, jnp.bfloat16, jnp.int32
HIGHEST = lax.Precision.HIGHEST

RMS_EPS = 1e-6
D_MODEL = 1024
D_MLSTM = 512
D_S5 = 512
HEADS = 4
HEAD_DIM = 128
CONV_WIDTH = 4
S5_GROUP = 16
S5_GROUPS = 32
S5_STATE = 64
N_EXPERTS = 8
N_GATE_COLS = 2 * HEADS

LANES = 128
SUBLANES = 8
VMEM_LIMIT = 56 * 1024 * 1024

S5_FOLD = SUBLANES
S5_GROUPS_PER_ROW = S5_GROUPS // S5_FOLD
S5_HALF = S5_GROUPS_PER_ROW * S5_STATE
S5_CH_PER_ROW = S5_GROUPS_PER_ROW * S5_GROUP
S5_SLABS = D_S5 // LANES
S5_C_ROWS = 4


def _params(sem):
    return pltpu.CompilerParams(dimension_semantics=sem, vmem_limit_bytes=VMEM_LIMIT)


def _const_spec(shape):
    nd = len(shape)
    return pl.BlockSpec(shape, lambda *_: (0,) * nd, pipeline_mode=pl.Buffered(1))


def _rmsnorm(x, g):
    y = x * lax.rsqrt(jnp.mean(x * x, axis=-1, keepdims=True) + RMS_EPS)
    return y * g


def _log_sigmoid(x):
    return jnp.minimum(x, 0.0) - jnp.log1p(jnp.exp(-jnp.abs(x)))


def _inproj_kernel(x_ref, g_ref, wqk_ref, wv_ref, wo_ref, wu_ref, wg_ref, bg_ref,
                   qk_ref, v_ref, o_ref, u_ref, gate_ref):
    hn = _rmsnorm(x_ref[...], g_ref[...]).astype(BF16)
    qk_ref[...] = jnp.dot(hn, wqk_ref[...], preferred_element_type=F32).astype(BF16)
    v_ref[...] = jnp.dot(hn, wv_ref[...], preferred_element_type=F32).astype(BF16)
    o_ref[...] = jnp.dot(hn, wo_ref[...], preferred_element_type=F32).astype(BF16)
    u_ref[...] = jnp.dot(hn, wu_ref[...], preferred_element_type=F32).astype(BF16)
    gp = jnp.dot(hn, wg_ref[...], preferred_element_type=F32) + bg_ref[...]
    lane = lax.broadcasted_iota(I32, gp.shape, 1)
    gate_ref[...] = jnp.where(lane < HEADS, gp, _log_sigmoid(gp))


def _inproj(x2, g, wqk, wv, wo, wu, wg, bg, tm):
    t = x2.shape[0]
    row = lambda n: pl.BlockSpec((tm, n), lambda i: (i, 0))
    return pl.pallas_call(
        _inproj_kernel,
        grid=(t // tm,),
        in_specs=[row(D_MODEL), _const_spec((1, D_MODEL)), _const_spec(wqk.shape), _const_spec(wv.shape),
                  _const_spec(wo.shape), _const_spec(wu.shape), _const_spec(wg.shape), _const_spec((1, LANES))],
        out_specs=[row(2 * D_MLSTM), row(D_MLSTM), row(D_MLSTM), row(D_S5), row(LANES)],
        out_shape=[jax.ShapeDtypeStruct((t, 2 * D_MLSTM), BF16), jax.ShapeDtypeStruct((t, D_MLSTM), BF16),
                   jax.ShapeDtypeStruct((t, D_MLSTM), BF16), jax.ShapeDtypeStruct((t, D_S5), BF16),
                   jax.ShapeDtypeStruct((t, LANES), F32)],
        compiler_params=_params(("arbitrary",)),
        name="inproj",
    )(x2, g, wqk, wv, wo, wu, wg, bg)


CONV_ROWS = 32
HALO = SUBLANES


def _mlstm_kernel(qk_ref, v_ref, o_ref, g_ref, cw_ref, nw_ref, out_ref,
                  xbuf, qkact, cstate, mstate, *, nb, lt, chunk):
    i = pl.program_id(0)

    @pl.when(i == 0)
    def _init():
        xbuf[:, 0:HALO, :] = jnp.zeros((nb, HALO, 2 * D_MLSTM), F32)
        cstate[...] = jnp.zeros(cstate.shape, F32)
        mstate[...] = jnp.zeros(mstate.shape, F32)

    lane = lax.broadcasted_iota(I32, (1, 2 * D_MLSTM), 1)
    qk_scale = jnp.where(lane < D_MLSTM, 1.0, HEAD_DIM ** -0.5).astype(F32)
    for b in range(nb):
        xbuf[b, HALO:HALO + lt, :] = qk_ref[b].astype(F32)

        def conv_block(r, carry, b=b):
            r0 = pl.multiple_of(r * CONV_ROWS, CONV_ROWS)
            win = xbuf[b, pl.ds(r0, CONV_ROWS + HALO), :]
            acc = win[HALO:, :] * cw_ref[CONV_WIDTH - 1:CONV_WIDTH, :]
            for j in range(1, CONV_WIDTH):
                acc = acc + win[HALO - j:HALO - j + CONV_ROWS, :] * cw_ref[CONV_WIDTH - 1 - j:CONV_WIDTH - j, :]
            act = acc * jax.nn.sigmoid(acc) * qk_scale
            qkact[b, pl.ds(r0, CONV_ROWS), :] = act.astype(BF16)
            return carry

        lax.fori_loop(0, lt // CONV_ROWS, conv_block, 0)
        xbuf[b, 0:HALO, :] = xbuf[b, lt:lt + HALO, :]

    L = chunk
    rr = lax.broadcasted_iota(I32, (L, L), 0)
    cc = lax.broadcasted_iota(I32, (L, L), 1)
    causal = rr >= cc
    tri = causal.astype(F32)
    ones_col = (lax.broadcasted_iota(I32, (L, HEAD_DIM), 1) == 0).astype(BF16)

    def chunk_body(c, carry):
        r0 = pl.multiple_of(c * L, L)
        for b in range(nb):
            gates = g_ref[b, pl.ds(r0, L), :]
            csum = jnp.dot(tri, gates, preferred_element_type=F32, precision=HIGHEST)
            gates_t = gates.T
            csum_t = csum.T
            for h in range(HEADS):
                sl = slice(h * HEAD_DIM, (h + 1) * HEAD_DIM)
                slk = slice(D_MLSTM + h * HEAD_DIM, D_MLSTM + (h + 1) * HEAD_DIM)
                li_col = gates[:, h:h + 1]
                li_row = gates_t[h:h + 1, :]
                b_col = csum[:, HEADS + h:HEADS + h + 1]
                b_row = csum_t[HEADS + h:HEADS + h + 1, :]
                m_prev = mstate[b * HEADS + h, 0:1, 0:1]

                log_d = jnp.where(causal, b_col - b_row + li_row, -jnp.inf)
                m_inter = b_col + m_prev
                m_t = jnp.maximum(m_inter, jnp.max(log_d, axis=-1, keepdims=True))
                d = jnp.exp(log_d - m_t)
                inter = jnp.exp(m_inter - m_t)

                q = qkact[b, pl.ds(r0, L), sl]
                k = qkact[b, pl.ds(r0, L), slk]
                v = v_ref[b, pl.ds(r0, L), sl]
                vext = jnp.concatenate([v, ones_col], axis=1)
                s = lax.dot_general(q, k, (((1,), (1,)), ((), ())), preferred_element_type=F32)
                p = (s * d).astype(BF16)
                cext = cstate[b * HEADS + h]
                r = (jnp.dot(p, vext, preferred_element_type=F32)
                     + inter * jnp.dot(q, cext.astype(BF16), preferred_element_type=F32))
                num = r[:, :HEAD_DIM]
                den = r[:, HEAD_DIM:HEAD_DIM + 1]
                rinv = 1.0 / jnp.maximum(jnp.abs(den), jnp.exp(-m_t))
                ms = jnp.mean(num * num, axis=-1, keepdims=True)
                hh = num * (rinv * lax.rsqrt(rinv * rinv * ms + RMS_EPS)) * nw_ref[:, sl]
                og = jax.nn.sigmoid(o_ref[b, pl.ds(r0, L), sl].astype(F32))
                out_ref[b, pl.ds(r0, L), sl] = (og * hh).astype(BF16)

                b_end = b_col[L - 1:L, :]
                log_w = b_end - b_col + li_col
                m_new = jnp.maximum(b_end + m_prev, jnp.max(log_w, axis=0, keepdims=True))
                w = jnp.exp(log_w - m_new)
                decay = jnp.exp(b_end + m_prev - m_new)
                kw = (k.astype(F32) * w).astype(BF16)
                upd = lax.dot_general(kw, vext, (((0,), (0,)), ((), ())), preferred_element_type=F32)
                cstate[b * HEADS + h] = decay * cext + upd
                mstate[b * HEADS + h] = jnp.broadcast_to(m_new, (SUBLANES, LANES))
        return carry

    lax.fori_loop(0, lt // L, chunk_body, 0)


def _mlstm(qk, v, o, gates, conv_w, norm_w, lt, chunk):
    nb, s, _ = qk.shape
    blk = lambda n: pl.BlockSpec((nb, lt, n), lambda i: (0, i, 0))
    kern = functools.partial(_mlstm_kernel, nb=nb, lt=lt, chunk=chunk)
    return pl.pallas_call(
        kern,
        grid=(s // lt,),
        in_specs=[blk(2 * D_MLSTM), blk(D_MLSTM), blk(D_MLSTM), blk(LANES),
                  _const_spec((CONV_WIDTH, 2 * D_MLSTM)), _const_spec((1, D_MLSTM))],
        out_specs=blk(D_MLSTM),
        out_shape=jax.ShapeDtypeStruct((nb, s, D_MLSTM), BF16),
        scratch_shapes=[pltpu.VMEM((nb, lt + HALO, 2 * D_MLSTM), F32),
                        pltpu.VMEM((nb, lt, 2 * D_MLSTM), BF16),
                        pltpu.VMEM((nb * HEADS, HEAD_DIM, 2 * HEAD_DIM), F32),
                        pltpu.VMEM((nb * HEADS, SUBLANES, LANES), F32)],
        compiler_params=_params(("arbitrary",)),
        name="mlstm",
    )(qk, v, o, gates, conv_w, norm_w)


S5_SUB = 128
S5_UNROLL = 8


def _gelu_tanh(x):
    return 0.5 * x * (1.0 + jnp.tanh(0.7978845608028654 * (x + 0.044715 * (x * x * x))))


def _s5_kernel(u_ref, wb_ref, are_ref, aim_ref, wc_ref, dsk_ref, wglu_ref, nw_ref, out_ref,
               bu, st, *, nb, lt):
    i = pl.program_id(0)

    @pl.when(i == 0)
    def _init():
        st[...] = jnp.zeros(st.shape, F32)

    rows = S5_SUB * S5_FOLD
    nsub = lt // S5_SUB
    pairs = S5_FOLD // S5_SLABS

    for b in range(nb):
        def bproj(s, carry, b=b):
            t0 = pl.multiple_of(s * S5_SUB, S5_SUB)
            base = pl.multiple_of(s * rows, rows)
            for q in range(S5_SLABS):
                uq = u_ref[b, pl.ds(t0, S5_SUB), q * LANES:(q + 1) * LANES]
                for r in range(pairs * q, pairs * (q + 1)):
                    res = jnp.dot(uq, wb_ref[r], preferred_element_type=F32)
                    for k in range(S5_SLABS):
                        bu[b, k, pl.ds(base + r, S5_SUB, stride=S5_FOLD), :] = res[:, k * LANES:(k + 1) * LANES]
            return carry
        lax.fori_loop(0, nsub, bproj, 0)

    nh = S5_SLABS // 2
    a_re = [are_ref[:, j * LANES:(j + 1) * LANES] for j in range(nh)]
    a_im = [aim_ref[:, j * LANES:(j + 1) * LANES] for j in range(nh)]

    def step(t, carry):
        r0 = pl.multiple_of(t * S5_FOLD, S5_FOLD)
        new = []
        for b in range(nb):
            nxt = [None] * S5_SLABS
            for j in range(nh):
                x_re, x_im = carry[b][j], carry[b][nh + j]
                n_re = a_re[j] * x_re - a_im[j] * x_im + bu[b, j, pl.ds(r0, S5_FOLD), :]
                n_im = a_re[j] * x_im + a_im[j] * x_re + bu[b, nh + j, pl.ds(r0, S5_FOLD), :]
                bu[b, j, pl.ds(r0, S5_FOLD), :] = n_re
                bu[b, nh + j, pl.ds(r0, S5_FOLD), :] = n_im
                nxt[j], nxt[nh + j] = n_re, n_im
            new.append(tuple(nxt))
        return tuple(new)

    init = tuple(tuple(st[b, k] for k in range(S5_SLABS)) for b in range(nb))
    fin = lax.fori_loop(0, lt, step, init, unroll=S5_UNROLL)
    for b in range(nb):
        for k in range(S5_SLABS):
            st[b, k] = fin[b][k]

    for b in range(nb):
        def cproj(s, carry, b=b):
            t0 = pl.multiple_of(s * S5_SUB, S5_SUB)
            base = pl.multiple_of(s * rows, rows)
            cols = []
            for q in range(S5_FOLD // S5_C_ROWS):
                acc = None
                for r in range(S5_C_ROWS * q, S5_C_ROWS * (q + 1)):
                    xr = jnp.concatenate(
                        [bu[b, k, pl.ds(base + r, S5_SUB, stride=S5_FOLD), :] for k in range(S5_SLABS)],
                        axis=1).astype(BF16)
                    part = jnp.dot(xr, wc_ref[r], preferred_element_type=F32)
                    acc = part if acc is None else acc + part
                cols.append(acc)
            y = jnp.concatenate(cols, axis=1)
            y = y + dsk_ref[...] * u_ref[b, pl.ds(t0, S5_SUB), :].astype(F32)
            g = _gelu_tanh(y)
            z = jnp.dot(g.astype(BF16), wglu_ref[...], preferred_element_type=F32)
            ys = g * jax.nn.sigmoid(z)
            ys = ys * lax.rsqrt(jnp.mean(ys * ys, axis=-1, keepdims=True) + RMS_EPS) * nw_ref[...]
            out_ref[b, pl.ds(t0, S5_SUB), :] = ys.astype(BF16)
            return carry
        lax.fori_loop(0, nsub, cproj, 0)


def _s5(u, wb, a_re, a_im, wc, dskip, wglu, norm_w, lt):
    nb, s, _ = u.shape
    blk = pl.BlockSpec((nb, lt, D_S5), lambda i: (0, i, 0))
    kern = functools.partial(_s5_kernel, nb=nb, lt=lt)
    return pl.pallas_call(
        kern,
        grid=(s // lt,),
        in_specs=[blk, _const_spec(wb.shape), _const_spec((S5_FOLD, S5_HALF)), _const_spec((S5_FOLD, S5_HALF)),
                  _const_spec(wc.shape), _const_spec((1, D_S5)), _const_spec((D_S5, D_S5)),
                  _const_spec((1, D_S5))],
        out_specs=blk,
        out_shape=jax.ShapeDtypeStruct((nb, s, D_S5), BF16),
        scratch_shapes=[pltpu.VMEM((nb, S5_SLABS, lt * S5_FOLD, LANES), F32),
                        pltpu.VMEM((nb, S5_SLABS, S5_FOLD, LANES), F32)],
        compiler_params=_params(("arbitrary",)),
        name="s5",
    )(u, wb, a_re, a_im, wc, dskip, wglu, norm_w)


def _s5_discretise(a_re, a_im, log_dt, b_re, b_im, c_re, c_im):
    dt = jnp.exp(log_dt)[:, None]
    mag = jnp.exp(a_re * dt)
    ab_re = mag * jnp.cos(a_im * dt)
    ab_im = mag * jnp.sin(a_im * dt)
    lam2 = a_re * a_re + a_im * a_im
    z_re = ab_re - 1.0
    f_re = (z_re * a_re + ab_im * a_im) / lam2
    f_im = (ab_im * a_re - z_re * a_im) / lam2
    bb_re = f_re[..., None] * b_re - f_im[..., None] * b_im
    bb_im = f_re[..., None] * b_im + f_im[..., None] * b_re

    gl = S5_GROUPS_PER_ROW
    a_re_t = ab_re.reshape(S5_FOLD, S5_HALF)
    a_im_t = ab_im.reshape(S5_FOLD, S5_HALF)

    def pack_b(bb):
        x = bb.reshape(S5_FOLD, gl, S5_STATE, S5_GROUP).transpose(0, 1, 3, 2)
        eye = jnp.eye(gl, dtype=F32)
        return jnp.einsum('rghp,gk->rghkp', x, eye).reshape(D_S5, S5_HALF)
    wb = jnp.concatenate([pack_b(bb_re), pack_b(bb_im)], axis=1)

    def pack_c(c):
        x = c.reshape(S5_FOLD, gl, S5_GROUP, S5_STATE)
        eye = jnp.eye(gl, dtype=F32)
        return jnp.einsum('rghp,gk->kprgh', x, eye).reshape(S5_HALF, D_S5)
    wc = jnp.concatenate([pack_c(c_re), -pack_c(c_im)], axis=0)

    pairs = S5_FOLD // S5_SLABS
    place_b = jnp.eye(pairs, dtype=F32)[jnp.arange(S5_FOLD) % pairs]
    wb_r = wb.reshape(S5_FOLD, S5_CH_PER_ROW, D_S5)
    wb_r = jnp.einsum('rcl,rp->rpcl', wb_r, place_b).reshape(S5_FOLD, LANES, D_S5)
    place_c = jnp.eye(S5_C_ROWS, dtype=F32)[jnp.arange(S5_FOLD) % S5_C_ROWS]
    wc_r = wc.reshape(D_S5, S5_FOLD, S5_CH_PER_ROW).transpose(1, 0, 2)
    wc_r = jnp.einsum('rlc,rp->rlpc', wc_r, place_c).reshape(S5_FOLD, D_S5, S5_C_ROWS * S5_CH_PER_ROW)
    return wb_r.astype(BF16), a_re_t, a_im_t, wc_r.astype(BF16)


def _outproj_kernel(x_ref, hm_ref, ys_ref, w1_ref, w2_ref, out_ref):
    out_ref[...] = (x_ref[...]
                    + jnp.dot(hm_ref[...], w1_ref[...], preferred_element_type=F32)
                    + jnp.dot(ys_ref[...], w2_ref[...], preferred_element_type=F32))


def _outproj(x2, hm, ys, w1, w2, tm):
    t = x2.shape[0]
    row = lambda n: pl.BlockSpec((tm, n), lambda i: (i, 0))
    return pl.pallas_call(
        _outproj_kernel,
        grid=(t // tm,),
        in_specs=[row(D_MODEL), row(D_MLSTM), row(D_S5), _const_spec(w1.shape), _const_spec(w2.shape)],
        out_specs=row(D_MODEL),
        out_shape=jax.ShapeDtypeStruct((t, D_MODEL), F32),
        compiler_params=_params(("arbitrary",)),
        name="outproj",
    )(x2, hm, ys, w1, w2)


FF_CHUNK = 256


def _ffn_kernel(x_ref, g_ref, wg_ref, wu_ref, wd_ref, out_ref, hn_s, *, d_ff):
    x = x_ref[...]
    hn_s[...] = _rmsnorm(x, g_ref[...]).astype(BF16)
    out_ref[...] = x
    for c in range(d_ff // FF_CHUNK):
        cs = slice(c * FF_CHUNK, (c + 1) * FF_CHUNK)
        hn = hn_s[...]
        a = jnp.dot(hn, wg_ref[:, cs], preferred_element_type=F32)
        b = jnp.dot(hn, wu_ref[:, cs], preferred_element_type=F32)
        h = (a * jax.nn.sigmoid(a) * b).astype(BF16)
        out_ref[...] += jnp.dot(h, wd_ref[cs, :], preferred_element_type=F32)


def _ffn(x2, g, wg, wu, wd, tm):
    t = x2.shape[0]
    d_ff = wg.shape[1]
    row = pl.BlockSpec((tm, D_MODEL), lambda i: (i, 0))
    return pl.pallas_call(
        functools.partial(_ffn_kernel, d_ff=d_ff),
        grid=(t // tm,),
        in_specs=[row, _const_spec((1, D_MODEL)), _const_spec(wg.shape), _const_spec(wu.shape),
                  _const_spec(wd.shape)],
        out_specs=row,
        out_shape=jax.ShapeDtypeStruct((t, D_MODEL), F32),
        scratch_shapes=[pltpu.VMEM((tm, D_MODEL), BF16)],
        compiler_params=_params(("arbitrary",)),
        name="ffn_dense",
    )(x2, g, wg, wu, wd)


def _router_kernel(x_ref, g_ref, wrt_ref, br_ref, hn_ref, comb_ref, pos_ref, cnt_ref, tok_ref, *, tb):
    hn = _rmsnorm(x_ref[...], g_ref[...])
    hn_ref[...] = hn.astype(BF16)
    logits = lax.dot_general(wrt_ref[...], hn, (((1,), (1,)), ((), ())),
                             preferred_element_type=F32, precision=HIGHEST) + br_ref[...]
    eidx = lax.broadcasted_iota(I32, logits.shape, 0)
    v1 = jnp.max(logits, axis=0, keepdims=True)
    i1 = jnp.min(jnp.where(logits == v1, eidx, N_EXPERTS), axis=0, keepdims=True)
    m1 = eidx == i1
    rest = jnp.where(m1, -jnp.inf, logits)
    v2 = jnp.max(rest, axis=0, keepdims=True)
    i2 = jnp.min(jnp.where(rest == v2, eidx, N_EXPERTS), axis=0, keepdims=True)
    m2 = eidx == i2
    e2 = jnp.exp(v2 - v1)
    g1 = 1.0 / (1.0 + e2)
    g2 = e2 / (1.0 + e2)
    comb_ref[...] = jnp.where(m1, g1, 0.0) + jnp.where(m2, g2, 0.0)
    sel = m1 | m2
    tr = lax.broadcasted_iota(I32, (tb, tb), 0)
    tc = lax.broadcasted_iota(I32, (tb, tb), 1)
    upper = (tr <= tc).astype(BF16)
    incl = jnp.dot(sel.astype(BF16), upper, preferred_element_type=F32)
    pos = incl - 1.0
    pos_ref[...] = jnp.where(sel, pos.astype(I32), -1)
    cnt_ref[0] = jnp.broadcast_to(incl[:, tb - 1:tb], (N_EXPERTS, LANES)).astype(I32)
    p1 = jnp.sum(jnp.where(m1, pos, 0.0), axis=0, keepdims=True)
    p2 = jnp.sum(jnp.where(m2, pos, 0.0), axis=0, keepdims=True)
    ri = lax.broadcasted_iota(I32, (LANES, tb), 0)
    info = jnp.where(ri == 0, p1, jnp.where(ri == 1, p2, jnp.where(
        ri == 2, i1.astype(F32), jnp.where(ri == 3, i2.astype(F32), 0.0))))
    tok_ref[...] = info.T


def _router(x2, g, wrt, br, tb):
    t = x2.shape[0]
    nblk = t // tb
    return pl.pallas_call(
        functools.partial(_router_kernel, tb=tb),
        grid=(nblk,),
        in_specs=[pl.BlockSpec((tb, D_MODEL), lambda i: (i, 0)), _const_spec((1, D_MODEL)),
                  _const_spec((N_EXPERTS, D_MODEL)), _const_spec((N_EXPERTS, 1))],
        out_specs=[pl.BlockSpec((tb, D_MODEL), lambda i: (i, 0)),
                   pl.BlockSpec((N_EXPERTS, tb), lambda i: (0, i)),
                   pl.BlockSpec((N_EXPERTS, tb), lambda i: (0, i)),
                   pl.BlockSpec((1, N_EXPERTS, LANES), lambda i: (i, 0, 0)),
                   pl.BlockSpec((tb, LANES), lambda i: (i, 0))],
        out_shape=[jax.ShapeDtypeStruct((t, D_MODEL), BF16), jax.ShapeDtypeStruct((N_EXPERTS, t), F32),
                   jax.ShapeDtypeStruct((N_EXPERTS, t), I32),
                   jax.ShapeDtypeStruct((nblk, N_EXPERTS, LANES), I32),
                   jax.ShapeDtypeStruct((t, LANES), F32)],
        compiler_params=_params(("arbitrary",)),
        name="router",
    )(x2, g, wrt, br)


SLOT_TILE = 128
COMBINE_ROWS = 256


def _moe_kernel(cnt_ref, x_ref, hn_ref, pos_ref, comb_ref, tok_ref, wg_ref, wu_ref, wd_ref, out_ref,
                xe, ye, yall, *, tb, nf):
    i, e, f = pl.program_id(0), pl.program_id(1), pl.program_id(2)
    tiles_of = lambda k: (cnt_ref[i * N_EXPERTS + k] + SLOT_TILE - 1) // SLOT_TILE
    ntiles = tiles_of(e)
    seg_start = [jnp.int32(0)]
    for k in range(N_EXPERTS - 1):
        seg_start.append(seg_start[-1] + tiles_of(k) * SLOT_TILE)
    off_e = jnp.int32(0)
    for k in range(1, N_EXPERTS):
        off_e = jnp.where(e == k, seg_start[k], off_e)
    pos_row = pos_ref[pl.ds(e, 1), :]
    slot_iota = lax.broadcasted_iota(I32, (SLOT_TILE, tb), 0)

    @pl.when((e == 0) & (f == 0))
    def _clear():
        yall[...] = jnp.zeros(yall.shape, BF16)

    @pl.when(f == 0)
    def _gather():
        def body(j, carry):
            j0 = pl.multiple_of(j * SLOT_TILE, SLOT_TILE)
            onehot = (pos_row == slot_iota + j0).astype(BF16)
            xe[pl.ds(j0, SLOT_TILE), :] = jnp.dot(
                onehot, hn_ref[...], preferred_element_type=F32).astype(BF16)
            ye[pl.ds(j0, SLOT_TILE), :] = jnp.zeros((SLOT_TILE, D_MODEL), F32)
            return carry
        lax.fori_loop(0, ntiles, body, 0)

    def expert(j, carry):
        j0 = pl.multiple_of(j * SLOT_TILE, SLOT_TILE)
        xs = xe[pl.ds(j0, SLOT_TILE), :]
        a = jnp.dot(xs, wg_ref[0], preferred_element_type=F32)
        b = jnp.dot(xs, wu_ref[0], preferred_element_type=F32)
        h = (a * jax.nn.sigmoid(a) * b).astype(BF16)
        ye[pl.ds(j0, SLOT_TILE), :] += jnp.dot(h, wd_ref[0], preferred_element_type=F32)
        return carry
    lax.fori_loop(0, ntiles, expert, 0)

    @pl.when(f == nf - 1)
    def _scale():
        comb_row = comb_ref[pl.ds(e, 1), :]

        def body(j, carry):
            j0 = pl.multiple_of(j * SLOT_TILE, SLOT_TILE)
            hit = pos_row == slot_iota + j0
            gate = jnp.sum(jnp.where(hit, comb_row, 0.0), axis=1, keepdims=True)
            dst = pl.multiple_of(off_e + j0, SLOT_TILE)
            yall[pl.ds(dst, SLOT_TILE), :] = (ye[pl.ds(j0, SLOT_TILE), :] * gate).astype(BF16)
            return carry
        lax.fori_loop(0, ntiles, body, 0)

    @pl.when((e == N_EXPERTS - 1) & (f == nf - 1))
    def _combine():
        smax = yall.shape[0]
        lane_slot = lax.broadcasted_iota(I32, (COMBINE_ROWS, smax), 1).astype(F32)

        def body(r, carry):
            t0 = pl.multiple_of(r * COMBINE_ROWS, COMBINE_ROWS)
            info = tok_ref[pl.ds(t0, COMBINE_ROWS), :]
            s1, s2 = info[:, 0:1], info[:, 1:2]
            i1, i2 = info[:, 2:3], info[:, 3:4]
            for k in range(1, N_EXPERTS):
                start = seg_start[k].astype(F32)
                s1 = s1 + jnp.where(i1 == float(k), start, 0.0)
                s2 = s2 + jnp.where(i2 == float(k), start, 0.0)
            onehot = ((lane_slot == s1) | (lane_slot == s2)).astype(BF16)
            out_ref[pl.ds(t0, COMBINE_ROWS), :] = x_ref[pl.ds(t0, COMBINE_ROWS), :] + jnp.dot(
                onehot, yall[...], preferred_element_type=F32)
            return carry
        lax.fori_loop(0, tb // COMBINE_ROWS, body, 0)


def _moe(x2, hn, pos, comb, cnt, tokinfo, wg, wu, wd, tb, nf):
    t = x2.shape[0]
    d_ff = wg.shape[2]
    fc = d_ff // nf
    smax = 2 * tb + N_EXPERTS * SLOT_TILE
    single = dict(pipeline_mode=pl.Buffered(1))
    tok = lambda n: pl.BlockSpec((tb, n), lambda i, e, f, c: (i, 0), **single)
    rowinfo = pl.BlockSpec((N_EXPERTS, tb), lambda i, e, f, c: (0, i))
    grid_spec = pltpu.PrefetchScalarGridSpec(
        num_scalar_prefetch=1,
        grid=(t // tb, N_EXPERTS, nf),
        in_specs=[tok(D_MODEL), tok(D_MODEL), rowinfo, rowinfo, tok(LANES),
                  pl.BlockSpec((1, D_MODEL, fc), lambda i, e, f, c: (e, 0, f)),
                  pl.BlockSpec((1, D_MODEL, fc), lambda i, e, f, c: (e, 0, f)),
                  pl.BlockSpec((1, fc, D_MODEL), lambda i, e, f, c: (e, f, 0))],
        out_specs=pl.BlockSpec((tb, D_MODEL), lambda i, e, f, c: (i, 0)),
        scratch_shapes=[pltpu.VMEM((tb, D_MODEL), BF16), pltpu.VMEM((tb, D_MODEL), F32),
                        pltpu.VMEM((smax, D_MODEL), BF16)],
    )
    return pl.pallas_call(
        functools.partial(_moe_kernel, tb=tb, nf=nf),
        grid_spec=grid_spec,
        out_shape=jax.ShapeDtypeStruct((t, D_MODEL), F32),
        compiler_params=_params(("arbitrary", "arbitrary", "arbitrary")),
        name="moe_experts",
    )(cnt, x2, hn, pos, comb, tokinfo, wg, wu, wd)


def _final_norm_kernel(x_ref, g_ref, out_ref):
    out_ref[...] = _rmsnorm(x_ref[...], g_ref[...])


def _final_norm(x2, g, tm):
    t = x2.shape[0]
    row = pl.BlockSpec((tm, D_MODEL), lambda i: (i, 0))
    return pl.pallas_call(
        _final_norm_kernel,
        grid=(t // tm,),
        in_specs=[row, _const_spec((1, D_MODEL))],
        out_specs=row,
        out_shape=jax.ShapeDtypeStruct((t, D_MODEL), F32),
        compiler_params=_params(("arbitrary",)),
        name="final_norm",
    )(x2, g)


def _tile(n, pref):
    return pref if n % pref == 0 else n


def kernel(x, norm_mix, w_in, conv_qk, b_igate, b_fgate, norm_mlstm, s5_a_re, s5_a_im, s5_log_dt, s5_b_re, s5_b_im, s5_c_re, s5_c_im, s5_d, w_glu, norm_s5, w_out, norm_ffn, ffn_w_gate, ffn_w_up, ffn_w_down, moe_w_router, moe_b_router, moe_w_gate, moe_w_up, moe_w_down, norm_final):
    nb, s, d = x.shape
    assert d == D_MODEL
    t = nb * s
    depth = w_in.shape[0]
    tm = _tile(t, 512)
    tm_proj = _tile(t, 1024)
    lt = _tile(s, 512)
    chunk = _tile(lt, 128)
    tb = _tile(t, 1024)
    nf = 2

    x2 = x.reshape(t, d)
    c0, c1, c2, c3 = D_MLSTM, 2 * D_MLSTM, 3 * D_MLSTM, 4 * D_MLSTM
    cu = c3 + N_GATE_COLS
    for l in range(depth):
        w = w_in[l]
        wg = jnp.zeros((d, LANES), F32).at[:, :N_GATE_COLS].set(w[:, c3:cu]).astype(BF16)
        bg = jnp.zeros((1, LANES), F32).at[0, :HEADS].set(b_igate[l]).at[0, HEADS:N_GATE_COLS].set(b_fgate[l])
        qk, v, o, u, gates = _inproj(
            x2, norm_mix[l][None], w[:, :c1].astype(BF16), w[:, c1:c2].astype(BF16),
            w[:, c2:c3].astype(BF16), w[:, cu:].astype(BF16), wg, bg, tm_proj)
        r3 = lambda a: a.reshape(nb, s, a.shape[-1])
        hm = _mlstm(r3(qk), r3(v), r3(o), r3(gates), conv_qk[l], norm_mlstm[l][None], lt, chunk)
        wb, a_re_t, a_im_t, wc = _s5_discretise(s5_a_re[l], s5_a_im[l], s5_log_dt[l], s5_b_re[l], s5_b_im[l],
                                                s5_c_re[l], s5_c_im[l])
        ys = _s5(r3(u), wb, a_re_t, a_im_t, wc, s5_d[l][None], w_glu[l].astype(BF16), norm_s5[l][None], lt)
        x2 = _outproj(x2, hm.reshape(t, D_MLSTM), ys.reshape(t, D_S5),
                      w_out[l, :D_MLSTM].astype(BF16), w_out[l, D_MLSTM:].astype(BF16), tm_proj)
        j = l // 2
        if l % 2 == 0:
            x2 = _ffn(x2, norm_ffn[l][None], ffn_w_gate[j].astype(BF16), ffn_w_up[j].astype(BF16),
                      ffn_w_down[j].astype(BF16), tm)
        else:
            hn, comb, pos, cnt, tokinfo = _router(x2, norm_ffn[l][None], moe_w_router[j].T,
                                                  moe_b_router[j][:, None], tb)
            x2 = _moe(x2, hn, pos, comb, cnt[:, :, 0].reshape(-1), tokinfo, moe_w_gate[j].astype(BF16),
                      moe_w_up[j].astype(BF16), moe_w_down[j].astype(BF16), tb, nf)
    out = _final_norm(x2, norm_final[None], tm)
    return out.reshape(nb, s, d)
```
